```python
import math
import jax, jax.numpy as jnp
from jax import lax
import numpy as np

D_MODEL = 2048
BATCH = 4
SEQ = 2048
DEPTH = 4
DEC_BATCH = 128
DEC_SEQ = 4
PAST_LEN = 8192
PAGE_SIZE = 128

N_MIXERS = 3
N_A_LAYERS = (DEPTH + 2) // 3
N_B_LAYERS = (DEPTH + 1) // 3
N_C_LAYERS = DEPTH // 3
EPS = 1e-6

A_HEADS = 16
A_NOPE = 128
A_ROPE = 64
A_VDIM = 128
A_Q_LORA = 512
A_KV_LORA = 512
A_WIDTH = A_HEADS * A_VDIM
A_IN = A_Q_LORA + A_KV_LORA + A_ROPE + A_WIDTH
A_SCALE = (A_NOPE + A_ROPE) ** -0.5
ROPE_THETA = 10000.0
Q_BLOCK = 128

B_WIDTH = 2 * D_MODEL
B_HEADDIM = 64
B_HEADS = B_WIDTH // B_HEADDIM
B_GROUPS = 8
B_STATE = 128
B_CONV = 4
B_CONV_DIM = B_WIDTH + 2 * B_GROUPS * B_STATE
B_IN = B_WIDTH + B_CONV_DIM + B_HEADS
B_CHUNK = 128

C_WIDTH = 2 * D_MODEL
C_GROUPS = 16
C_GDIM = C_WIDTH // C_GROUPS
C_CHUNK = 128
C_IN = 3 * C_WIDTH

kernel_name = 'hybrid_mla_ssd_gmlp_adaln_step'


def rmsnorm(x, w):
    xf = x.astype(jnp.float32)
    y = xf * lax.rsqrt(jnp.mean(xf * xf, axis=-1, keepdims=True) + EPS)
    return (y * w.astype(jnp.float32)).astype(x.dtype)


def layernorm(x, g, b):
    xf = x.astype(jnp.float32)
    mu = jnp.mean(xf, axis=-1, keepdims=True)
    var = jnp.mean(jnp.square(xf - mu), axis=-1, keepdims=True)
    y = (xf - mu) * lax.rsqrt(var + EPS) * g.astype(jnp.float32) + b.astype(jnp.float32)
    return y.astype(x.dtype)


def rope(x, pos):
    half = A_ROPE // 2
    inv = ROPE_THETA ** (-jnp.arange(half, dtype=jnp.float32) / half)
    ang = pos.astype(jnp.float32)[:, None] * inv[None, :]
    cos, sin = jnp.cos(ang)[:, None, :], jnp.sin(ang)[:, None, :]
    x1 = x[..., :half].astype(jnp.float32)
    x2 = x[..., half:].astype(jnp.float32)
    return jnp.concatenate([x1 * cos - x2 * sin, x1 * sin + x2 * cos], axis=-1).astype(x.dtype)


def mla_inputs(h, pos, w_in, q_norm, w_uq, kv_norm, w_ukv):
    bsz, L, _ = h.shape
    proj = h @ w_in
    o1, o2, o3 = A_Q_LORA, A_Q_LORA + A_KV_LORA, A_Q_LORA + A_KV_LORA + A_ROPE
    cq, ckv, kpe, gate = proj[..., :o1], proj[..., o1:o2], proj[..., o2:o3], proj[..., o3:]
    q = (rmsnorm(cq, q_norm) @ w_uq).reshape(bsz, L, A_HEADS, A_NOPE + A_ROPE)
    q_pe = rope(q[..., A_NOPE:], pos)
    q_lat = jnp.einsum('blhn,chn->blhc', q[..., :A_NOPE], w_ukv[..., :A_NOPE])
    ckv = rmsnorm(ckv, kv_norm)
    kpe = rope(kpe[:, :, None, :], pos)[:, :, 0]
    return q_lat, q_pe, ckv, kpe, gate


def latent_scores(q_lat, q_pe, ckv, kpe):
    s = jnp.einsum('bqhc,bkc->bhqk', q_lat, ckv, preferred_element_type=jnp.float32)
    s = s + jnp.einsum('bqhr,bkr->bhqk', q_pe, kpe, preferred_element_type=jnp.float32)
    return s * A_SCALE


def mla_prompt_attention(q_lat, q_pe, ckv, kpe):
    bsz, L = q_lat.shape[:2]
    nb = L // Q_BLOCK
    ql = q_lat.reshape(bsz, nb, Q_BLOCK, A_HEADS, A_KV_LORA).swapaxes(0, 1)
    qp = q_pe.reshape(bsz, nb, Q_BLOCK, A_HEADS, A_ROPE).swapaxes(0, 1)
    kpos = jnp.arange(L)

    def block(args):
        ql_b, qp_b, i = args
        s = latent_scores(ql_b, qp_b, ckv, kpe)
        qpos = i * Q_BLOCK + jnp.arange(Q_BLOCK)
        s = jnp.where(kpos[None, :] <= qpos[:, None], s, -jnp.inf)
        p = jax.nn.softmax(s, axis=-1).astype(ckv.dtype)
        return jnp.einsum('bhqk,bkc->bqhc', p, ckv)

    o = lax.map(block, (ql, qp, jnp.arange(nb)))
    return o.swapaxes(0, 1).reshape(bsz, L, A_HEADS, A_KV_LORA)


def mla_sample_attention(q_lat, q_pe, ckv_new, kpe_new, ckv_past, kpe_past):
    Lq = q_lat.shape[1]
    s_past = latent_scores(q_lat, q_pe, ckv_past, kpe_past)
    s_new = latent_scores(q_lat, q_pe, ckv_new, kpe_new)
    s_new = jnp.where(jnp.tril(jnp.ones((Lq, Lq), dtype=bool)), s_new, -jnp.inf)
    p = jax.nn.softmax(jnp.concatenate([s_past, s_new], axis=-1), axis=-1).astype(ckv_new.dtype)
    n_past = ckv_past.shape[1]
    return (jnp.einsum('bhqk,bkc->bqhc', p[..., :n_past], ckv_past)
            + jnp.einsum('bhqk,bkc->bqhc', p[..., n_past:], ckv_new))


def mla_output(o_lat, gate, w_ukv, w_out):
    o = jnp.einsum('blhc,chv->blhv', o_lat, w_ukv[..., A_NOPE:])
    o = o.reshape(o.shape[0], o.shape[1], A_WIDTH)
    return (o * jax.nn.silu(gate)) @ w_out


def causal_conv(xpad, w, b):
    out = lax.conv_general_dilated(xpad, w[:, None, :], window_strides=(1,), padding='VALID',
                                   dimension_numbers=('NWC', 'WIO', 'NWC'),
                                   feature_group_count=xpad.shape[-1])
    return out + b


def ssd_scan(x, dt, A, Bm, Cm, h0):
    bsz, L = x.shape[:2]
    Q = min(B_CHUNK, L)
    nc = L // Q
    hpg = B_HEADS // B_GROUPS
    x = x.reshape(bsz, nc, Q, B_GROUPS, hpg, B_HEADDIM)
    dt = dt.reshape(bsz, nc, Q, B_GROUPS, hpg)
    Bm = Bm.reshape(bsz, nc, Q, B_GROUPS, B_STATE)
    Cm = Cm.reshape(bsz, nc, Q, B_GROUPS, B_STATE)
    acum = jnp.cumsum(dt * A.reshape(B_GROUPS, hpg), axis=2)
    ac = jnp.moveaxis(acum, 2, -1)
    diff = ac[..., :, None] - ac[..., None, :]
    decay = jnp.exp(jnp.where(jnp.tril(jnp.ones((Q, Q), dtype=bool)), diff, -jnp.inf))
    cb = jnp.einsum('bcqgn,bckgn->bcgqk', Cm, Bm)
    m = cb[:, :, :, None] * decay * jnp.moveaxis(dt, 2, -1)[..., None, :]
    y = jnp.einsum('bcghqk,bckghp->bcqghp', m, x)
    xw = x * (jnp.exp(acum[:, :, -1:] - acum) * dt)[..., None]
    states = jnp.einsum('bckgn,bckghp->bcghpn', Bm, xw)
    chunk_decay = jnp.exp(acum[:, :, -1])

    def step(h, inp):
        dec, st = inp
        return dec[..., None, None] * h + st, h

    h_last, h_in = lax.scan(step, h0.reshape(bsz, B_GROUPS, hpg, B_HEADDIM, B_STATE),
                            (jnp.moveaxis(chunk_decay, 1, 0), jnp.moveaxis(states, 1, 0)))
    h_in = jnp.moveaxis(h_in, 0, 1)
    y = y + jnp.einsum('bcqgn,bcghpn->bcqghp', Cm, h_in) * jnp.exp(acum)[..., None]
    return (y.reshape(bsz, L, B_HEADS, B_HEADDIM),
            h_last.reshape(bsz, B_HEADS, B_HEADDIM, B_STATE))


def ssd_mixer(h, conv_prev, h0, w_in, conv_w, conv_b, dt_bias, a_log, d_skip, norm_w, w_out):
    bsz, L, _ = h.shape
    f32 = jnp.float32
    proj = h @ w_in
    z = proj[..., :B_WIDTH]
    xbc = proj[..., B_WIDTH:B_WIDTH + B_CONV_DIM]
    dt = proj[..., B_WIDTH + B_CONV_DIM:]
    xpad = jnp.concatenate([conv_prev.astype(xbc.dtype), xbc], axis=1)
    conv_new = xpad[:, -(B_CONV - 1):]
    xbc = jax.nn.silu(causal_conv(xpad, conv_w, conv_b)).astype(f32)
    gn = B_GROUPS * B_STATE
    xs = xbc[..., :B_WIDTH].reshape(bsz, L, B_HEADS, B_HEADDIM)
    Bm = xbc[..., B_WIDTH:B_WIDTH + gn].reshape(bsz, L, B_GROUPS, B_STATE)
    Cm = xbc[..., B_WIDTH + gn:].reshape(bsz, L, B_GROUPS, B_STATE)
    dt = jax.nn.softplus(dt.astype(f32) + dt_bias.astype(f32))
    A = -jnp.exp(a_log.astype(f32))
    y, h_last = ssd_scan(xs, dt, A, Bm, Cm, h0.astype(f32))
    y = (y + d_skip.astype(f32)[:, None] * xs).reshape(bsz, L, B_WIDTH)
    y = y * jax.nn.silu(z.astype(f32))
    y = rmsnorm(y.reshape(bsz, L, B_GROUPS, B_WIDTH // B_GROUPS),
                norm_w.reshape(B_GROUPS, B_WIDTH // B_GROUPS)).reshape(bsz, L, B_WIDTH)
    return y.astype(h.dtype) @ w_out, conv_new, h_last


def chunk_mlp(h, w_in, ln_g, ln_b, ws, bs, w_out):
    bsz, L, _ = h.shape
    proj = h @ w_in
    u = jax.nn.gelu(proj[..., :C_WIDTH])
    v = layernorm(jax.nn.gelu(proj[..., C_WIDTH:2 * C_WIDTH]), ln_g, ln_b)
    g = proj[..., 2 * C_WIDTH:]
    Q = min(C_CHUNK, L)
    nc = L // Q
    w = jnp.tril(ws)[:, :Q, :Q]
    vb = v.reshape(bsz, nc, Q, C_GROUPS, C_GDIM)
    mix = jnp.einsum('gts,bnsgc->bntgc', w, vb) + bs[:, :Q].T[None, None, :, :, None]
    out = (u * mix.reshape(bsz, L, C_WIDTH) * jax.nn.silu(g)) @ w_out
    return out, v


def trunk(x, c, pos, prm, past):
    bsz = x.shape[0]
    ckv_rows, kpe_rows, ssm_out, conv_out, v_out = [], [], [], [], []
    for l in range(DEPTH):
        kind, j = l % N_MIXERS, l // N_MIXERS
        mod = jax.nn.silu(c) @ prm['ada_w'][l] + prm['ada_b'][l]
        shift, scale, gate = jnp.split(mod, 3, axis=-1)
        h = rmsnorm(x, prm['norm_w'][l]) * (1 + scale[:, None]) + shift[:, None]
        if kind == 0:
            q_lat, q_pe, ckv, kpe, g = mla_inputs(h, pos, prm['a_w_in'][j], prm['a_q_norm'][j],
                                                  prm['a_w_uq'][j], prm['a_kv_norm'][j], prm['a_w_ukv'][j])
            if past is None:
                o_lat = mla_prompt_attention(q_lat, q_pe, ckv, kpe)
            else:
                ckv_past = past[0][j, past[2]].reshape(bsz, -1, A_KV_LORA)
                kpe_past = past[1][j, past[2]].reshape(bsz, -1, A_ROPE)
                o_lat = mla_sample_attention(q_lat, q_pe, ckv, kpe, ckv_past, kpe_past)
            out = mla_output(o_lat, g, prm['a_w_ukv'][j], prm['a_w_out'][j])
            ckv_rows.append(ckv)
            kpe_rows.append(kpe)
        elif kind == 1:
            if past is None:
                conv_prev = jnp.zeros((bsz, B_CONV - 1, B_CONV_DIM), x.dtype)
                h0 = jnp.zeros((bsz, B_HEADS, B_HEADDIM, B_STATE), jnp.float32)
            else:
                conv_prev, h0 = past[4][j], past[3][j]
            out, conv_new, h_last = ssd_mixer(h, conv_prev, h0, prm['b_w_in'][j], prm['b_conv_w'][j],
                                              prm['b_conv_b'][j], prm['b_dt_bias'][j], prm['b_a_log'][j],
                                              prm['b_d'][j], prm['b_norm'][j], prm['b_w_out'][j])
            ssm_out.append(h_last)
            conv_out.append(conv_new)
        else:
            out, v = chunk_mlp(h, prm['c_w_in'][j], prm['c_ln_g'][j], prm['c_ln_b'][j],
                               prm['c_ws'][j], prm['c_bs'][j], prm['c_w_out'][j])
            v_out.append(v)
        x = x + gate[:, None] * out
    return rmsnorm(x, prm['norm_f']), ckv_rows, kpe_rows, ssm_out, conv_out, v_out


def setup_inputs(seed: int = 0) -> dict:
    key = jax.random.key(seed)
    k = jax.random.split(key, 36)
    f32 = jnp.float32
    n_pages = PAST_LEN // PAGE_SIZE
    n_pool = (DEC_BATCH * n_pages * 5) // 4

    def nrm(kk, shape, scale=1.0):
        return jax.random.normal(kk, shape, f32) * scale

    def gain(kk, shape):
        return 1.0 + 0.02 * jax.random.normal(kk, shape, f32)

    page_table = jax.random.permutation(k[6], n_pool)[:DEC_BATCH * n_pages]
    page_table = page_table.reshape(DEC_BATCH, n_pages).astype(jnp.int32)
    dt0 = jnp.exp(jax.random.uniform(k[23], (N_B_LAYERS, B_HEADS), f32, math.log(1e-3), math.log(1e-1)))
    b_dt_bias = dt0 + jnp.log(-jnp.expm1(-dt0))
    b_a_log = jnp.log(jax.random.uniform(k[24], (N_B_LAYERS, B_HEADS), f32, 1.0, 16.0))
    return {
        'x_prompt': nrm(k[0], (BATCH, SEQ, D_MODEL)),
        'x_sample': nrm(k[1], (DEC_BATCH, DEC_SEQ, D_MODEL)),
        'c_prompt': nrm(k[2], (BATCH, D_MODEL)),
        'c_sample': nrm(k[3], (DEC_BATCH, D_MODEL)),
        'cache_ckv': nrm(k[4], (N_A_LAYERS, n_pool, PAGE_SIZE, A_KV_LORA)),
        'cache_kpe': nrm(k[5], (N_A_LAYERS, n_pool, PAGE_SIZE, A_ROPE)),
        'page_table': page_table,
        'state_ssm': nrm(k[7], (N_B_LAYERS, DEC_BATCH, B_HEADS, B_HEADDIM, B_STATE), 0.5),
        'state_conv': nrm(k[8], (N_B_LAYERS, DEC_BATCH, B_CONV - 1, B_CONV_DIM)),
        'ada_w': nrm(k[9], (DEPTH, D_MODEL, 3 * D_MODEL), 0.5 * D_MODEL ** -0.5),
        'ada_b': nrm(k[10], (DEPTH, 3 * D_MODEL), 0.02),
        'norm_w': gain(k[11], (DEPTH, D_MODEL)),
        'norm_f': gain(k[12], (D_MODEL,)),
        'a_w_in': nrm(k[13], (N_A_LAYERS, D_MODEL, A_IN), D_MODEL ** -0.5),
        'a_q_norm': gain(k[14], (N_A_LAYERS, A_Q_LORA)),
        'a_w_uq': nrm(k[15], (N_A_LAYERS, A_Q_LORA, A_HEADS * (A_NOPE + A_ROPE)), A_Q_LORA ** -0.5),
        'a_kv_norm': gain(k[16], (N_A_LAYERS, A_KV_LORA)),
        'a_w_ukv': nrm(k[17], (N_A_LAYERS, A_KV_LORA, A_HEADS, A_NOPE + A_VDIM), A_KV_LORA ** -0.5),
        'a_w_out': nrm(k[18], (N_A_LAYERS, A_WIDTH, D_MODEL), A_WIDTH ** -0.5),
        'b_w_in': nrm(k[19], (N_B_LAYERS, D_MODEL, B_IN), D_MODEL ** -0.5),
        'b_conv_w': nrm(k[20], (N_B_LAYERS, B_CONV, B_CONV_DIM), B_CONV ** -0.5),
        'b_conv_b': nrm(k[21], (N_B_LAYERS, B_CONV_DIM), 0.02),
        'b_dt_bias': b_dt_bias,
        'b_a_log': b_a_log,
        'b_d': gain(k[22], (N_B_LAYERS, B_HEADS)),
        'b_norm': gain(k[25], (N_B_LAYERS, B_WIDTH)),
        'b_w_out': nrm(k[26], (N_B_LAYERS, B_WIDTH, D_MODEL), B_WIDTH ** -0.5),
        'c_w_in': nrm(k[27], (N_C_LAYERS, D_MODEL, C_IN), D_MODEL ** -0.5),
        'c_ln_g': gain(k[28], (N_C_LAYERS, C_WIDTH)),
        'c_ln_b': nrm(k[29], (N_C_LAYERS, C_WIDTH), 0.02),
        'c_ws': nrm(k[30], (N_C_LAYERS, C_GROUPS, C_CHUNK, C_CHUNK), C_CHUNK ** -0.5),
        'c_bs': gain(k[31], (N_C_LAYERS, C_GROUPS, C_CHUNK)),
        'c_w_out': nrm(k[32], (N_C_LAYERS, C_WIDTH, D_MODEL), C_WIDTH ** -0.5),
    }


def reference(x_prompt, x_sample, c_prompt, c_sample, cache_ckv, cache_kpe, page_table, state_ssm,
              state_conv, ada_w, ada_b, norm_w, norm_f, a_w_in, a_q_norm, a_w_uq, a_kv_norm, a_w_ukv,
              a_w_out, b_w_in, b_conv_w, b_conv_b, b_dt_bias, b_a_log, b_d, b_norm, b_w_out, c_w_in,
              c_ln_g, c_ln_b, c_ws, c_bs, c_w_out):
    prm = dict(ada_w=ada_w, ada_b=ada_b, norm_w=norm_w, norm_f=norm_f,
               a_w_in=a_w_in, a_q_norm=a_q_norm, a_w_uq=a_w_uq, a_kv_norm=a_kv_norm,
               a_w_ukv=a_w_ukv, a_w_out=a_w_out,
               b_w_in=b_w_in, b_conv_w=b_conv_w, b_conv_b=b_conv_b, b_dt_bias=b_dt_bias,
               b_a_log=b_a_log, b_d=b_d, b_norm=b_norm, b_w_out=b_w_out,
               c_w_in=c_w_in, c_ln_g=c_ln_g, c_ln_b=c_ln_b, c_ws=c_ws, c_bs=c_bs, c_w_out=c_w_out)
    pos_prompt = jnp.arange(SEQ, dtype=jnp.int32)
    pos_sample = PAST_LEN + jnp.arange(DEC_SEQ, dtype=jnp.int32)
    y_prompt, ckv_p, kpe_p, ssm_p, conv_p, _ = trunk(x_prompt, c_prompt, pos_prompt, prm, None)
    y_sample, ckv_s, kpe_s, ssm_s, conv_s, v_s = trunk(
        x_sample, c_sample, pos_sample, prm, (cache_ckv, cache_kpe, page_table, state_ssm, state_conv))
    return (y_prompt, y_sample,
            jnp.stack(ckv_p), jnp.stack(kpe_p), jnp.stack(ckv_s), jnp.stack(kpe_s),
            jnp.stack(ssm_p), jnp.stack(conv_p), jnp.stack(ssm_s), jnp.stack(conv_s),
            jnp.stack(v_s))
```

```python
import functools
import math

import jax
import jax.numpy as jnp
from jax import lax
from jax.experimental import pallas as pl
from jax.experimental.pallas import tpu as pltpu

F32 = jnp.float32
BF16 = jnp.bfloat16

D_MODEL = 2048
DEPTH = 4
EPS = 1e-6
A_HEADS = 16
A_NOPE = 128
A_ROPE = 64
A_VDIM = 128
A_Q_LORA = 512
A_KV_LORA = 512
A_WIDTH = A_HEADS * A_VDIM
A_SCALE = (A_NOPE + A_ROPE) ** -0.5
ROPE_THETA = 10000.0
PAGE_SIZE = 128
B_WIDTH = 2 * D_MODEL
B_HEADDIM = 64
B_HEADS = B_WIDTH // B_HEADDIM
B_GROUPS = 8
B_STATE = 128
B_CONV = 4
B_GN = B_GROUPS * B_STATE
B_CONV_DIM = B_WIDTH + 2 * B_GN
B_HPG = B_HEADS // B_GROUPS
B_GW = B_WIDTH // B_GROUPS
CHUNK = 128
C_WIDTH = 2 * D_MODEL
C_GROUPS = 16
C_GDIM = C_WIDTH // C_GROUPS

LANES = 128
SUBLANES = 8
VMEM_BYTES = 64 * 1024 * 1024

A_IN_PAD = 3200
A_QK = A_KV_LORA + LANES
B_IN_PAD = B_WIDTH + B_CONV_DIM + LANES
NEG = -1e30

NT_DIMS = (((1,), (1,)), ((), ()))
TN_DIMS = (((0,), (0,)), ((), ()))


def _cparams(sem, vmem_mb):
    return pltpu.CompilerParams(dimension_semantics=sem, vmem_limit_bytes=vmem_mb * 1024 * 1024)


def _silu(x):
    return x * (1.0 / (1.0 + jnp.exp(-x)))


def _gelu(x):
    return 0.5 * x * (1.0 + jnp.tanh(math.sqrt(2.0 / math.pi) * (x + 0.044715 * (x * x * x))))


def _softplus(x):
    return jnp.maximum(x, 0.0) + jnp.log1p(jnp.exp(-jnp.abs(x)))


def _rms(x, w):
    return x * lax.rsqrt(jnp.mean(x * x, axis=-1, keepdims=True) + EPS) * w


def _dot(a, b):
    return jnp.dot(a, b, preferred_element_type=F32)


def _dot_nt(a, b):
    return lax.dot_general(a, b, NT_DIMS, preferred_element_type=F32)


def _dot_tn(a, b):
    return lax.dot_general(a, b, TN_DIMS, preferred_element_type=F32)


def _split_dot(a, sel, parts, left=False):
    acc = None
    rem = a
    for _ in range(parts):
        piece = rem.astype(BF16)
        rem = rem - piece.astype(F32)
        term = _dot(sel, piece) if left else _dot(piece, sel)
        acc = term if acc is None else acc + term
    return acc


def _ada_kernel(c_ref, w_ref, b_ref, o_ref):
    s = _silu(c_ref[...]).astype(BF16)
    o_ref[0] = _dot(s, w_ref[0].astype(BF16)) + b_ref[0]


def _ada_call(c_all, ada_w, ada_b):
    n_rows = c_all.shape[0]
    tn = 1024
    return pl.pallas_call(
        _ada_kernel,
        grid=(DEPTH, 3 * D_MODEL // tn),
        in_specs=[
            pl.BlockSpec((n_rows, D_MODEL), lambda l, j: (0, 0)),
            pl.BlockSpec((1, D_MODEL, tn), lambda l, j: (l, 0, j)),
            pl.BlockSpec((1, 1, tn), lambda l, j: (l, 0, j)),
        ],
        out_specs=pl.BlockSpec((1, n_rows, tn), lambda l, j: (l, 0, j)),
        out_shape=jax.ShapeDtypeStruct((DEPTH, n_rows, 3 * D_MODEL), F32),
        compiler_params=_cparams(("parallel", "parallel"), 40),
        name="ada_mod",
    )(c_all, ada_w, ada_b.reshape(DEPTH, 1, 3 * D_MODEL))


def _in_kernel(x_ref, nw_ref, sh_ref, sc_ref, w_ref, o_ref, h_ref):
    @pl.when(pl.program_id(1) == 0)
    def _():
        y = _rms(x_ref[...], nw_ref[...])
        h_ref[...] = (y * (1.0 + sc_ref[0]) + sh_ref[0]).astype(BF16)

    o_ref[...] = _dot(h_ref[...], w_ref[...])


def _in_call(x, nw, mods, w, tm, tn, rows_per_mod):
    t, n = x.shape[0], w.shape[1]
    r = mods.shape[1]
    return pl.pallas_call(
        _in_kernel,
        grid=(t // tm, n // tn),
        in_specs=[
            pl.BlockSpec((tm, D_MODEL), lambda i, j: (i, 0)),
            pl.BlockSpec((1, D_MODEL), lambda i, j: (0, 0)),
            pl.BlockSpec((1, r, D_MODEL), lambda i, j: (i * tm // rows_per_mod, 0, 0)),
            pl.BlockSpec((1, r, D_MODEL), lambda i, j: (i * tm // rows_per_mod, 0, 1)),
            pl.BlockSpec((D_MODEL, tn), lambda i, j: (0, j)),
        ],
        out_specs=pl.BlockSpec((tm, tn), lambda i, j: (i, j)),
        out_shape=jax.ShapeDtypeStruct((t, n), F32),
        scratch_shapes=[pltpu.VMEM((tm, D_MODEL), BF16)],
        compiler_params=_cparams(("parallel", "arbitrary"), 56),
        name="in_proj",
    )(x, nw.reshape(1, D_MODEL), mods, mods, w)


def _out_kernel(a_ref, w_ref, x_ref, g_ref, o_ref):
    o_ref[...] = x_ref[...] + g_ref[0] * _dot(a_ref[...].astype(BF16), w_ref[...])


def _out_call(a, w, x, mods, tm, tn, rows_per_mod):
    t, k = a.shape
    r = mods.shape[1]
    nj = D_MODEL // tn
    return pl.pallas_call(
        _out_kernel,
        grid=(t // tm, nj),
        in_specs=[
            pl.BlockSpec((tm, k), lambda i, j: (i, 0)),
            pl.BlockSpec((k, tn), lambda i, j: (0, j)),
            pl.BlockSpec((tm, tn), lambda i, j: (i, j)),
            pl.BlockSpec((1, r, tn), lambda i, j: (i * tm // rows_per_mod, 0, 2 * nj + j)),
        ],
        out_specs=pl.BlockSpec((tm, tn), lambda i, j: (i, j)),
        out_shape=jax.ShapeDtypeStruct((t, D_MODEL), F32),
        compiler_params=_cparams(("parallel", "parallel"), 56),
        name="out_proj",
    )(a, w, x, mods)


def _norm_kernel(x_ref, w_ref, o_ref):
    o_ref[...] = _rms(x_ref[...], w_ref[...])


def _norm_call(x, w, tm):
    t = x.shape[0]
    return pl.pallas_call(
        _norm_kernel,
        grid=(t // tm,),
        in_specs=[pl.BlockSpec((tm, D_MODEL), lambda i: (i, 0)), pl.BlockSpec((1, D_MODEL), lambda i: (0, 0))],
        out_specs=pl.BlockSpec((tm, D_MODEL), lambda i: (i, 0)),
        out_shape=jax.ShapeDtypeStruct((t, D_MODEL), F32),
        compiler_params=_cparams(("parallel",), 40),
        name="final_norm",
    )(x, w.reshape(1, D_MODEL))


def _rope(x, cos, ssin):
    lane = lax.broadcasted_iota(jnp.int32, x.shape, 1)
    first = (lane % A_ROPE) < (A_ROPE // 2)
    partner = jnp.where(first, pltpu.roll(x, LANES - A_ROPE // 2, 1), pltpu.roll(x, A_ROPE // 2, 1))
    return x * cos + partner * ssin


def _qkv_kernel(cq_ref, ckv_ref, kpe_ref, cos_ref, sin_ref, qn_ref, kvn_ref, wuq_ref, wk_ref,
                qcat_ref, kcat_ref, ckv_o, kpe_o):
    cos, ssin = cos_ref[...], sin_ref[...]
    ckv = _rms(ckv_ref[...], kvn_ref[...])
    ckv_o[...] = ckv
    kpe = _rope(kpe_ref[...], cos, ssin)
    kpe_o[...] = kpe
    kcat_ref[:, 0:A_KV_LORA] = ckv.astype(BF16)
    kcat_ref[:, A_KV_LORA:A_QK] = kpe.astype(BF16)

    q = _dot(_rms(cq_ref[...], qn_ref[...]).astype(BF16), wuq_ref[...])
    for h in range(A_HEADS):
        qh = q[:, h * A_NOPE:(h + 1) * A_NOPE].astype(BF16)
        qcat_ref[h, :, 0:A_KV_LORA] = (_dot(qh, wk_ref[h]) * A_SCALE).astype(BF16)
    pe0 = A_HEADS * A_NOPE
    for pr in range(A_HEADS // 2):
        qp = _rope(q[:, pe0 + pr * LANES:pe0 + (pr + 1) * LANES], cos, ssin) * A_SCALE
        qcat_ref[2 * pr, :, A_KV_LORA:A_QK] = qp.astype(BF16)
        qcat_ref[2 * pr + 1, :, A_KV_LORA:A_QK] = pltpu.roll(qp, A_ROPE, 1).astype(BF16)


def _qkv_call(proj, cos, ssin, qn, kvn, wuq, wk, tm):
    t = proj.shape[0]
    cq_blk = D_MODEL // A_Q_LORA
    return pl.pallas_call(
        _qkv_kernel,
        grid=(t // tm,),
        in_specs=[
            pl.BlockSpec((tm, A_Q_LORA), lambda i: (i, cq_blk)),
            pl.BlockSpec((tm, A_KV_LORA), lambda i: (i, cq_blk + 1)),
            pl.BlockSpec((tm, LANES), lambda i: (i, (D_MODEL + A_Q_LORA + A_KV_LORA) // LANES)),
            pl.BlockSpec((tm, LANES), lambda i: (i, 0)),
            pl.BlockSpec((tm, LANES), lambda i: (i, 0)),
            pl.BlockSpec((1, A_Q_LORA), lambda i: (0, 0)),
            pl.BlockSpec((1, A_KV_LORA), lambda i: (0, 0)),
            pl.BlockSpec(wuq.shape, lambda i: (0, 0)),
            pl.BlockSpec(wk.shape, lambda i: (0, 0, 0)),
        ],
        out_specs=[
            pl.BlockSpec((A_HEADS, tm, A_QK), lambda i: (0, i, 0)),
            pl.BlockSpec((tm, A_QK), lambda i: (i, 0)),
            pl.BlockSpec((tm, A_KV_LORA), lambda i: (i, 0)),
            pl.BlockSpec((tm, LANES), lambda i: (i, 0)),
        ],
        out_shape=[
            jax.ShapeDtypeStruct((A_HEADS, t, A_QK), BF16),
            jax.ShapeDtypeStruct((t, A_QK), BF16),
            jax.ShapeDtypeStruct((t, A_KV_LORA), F32),
            jax.ShapeDtypeStruct((t, LANES), F32),
        ],
        compiler_params=_cparams(("parallel",), 48),
        name="mla_qkv",
    )(proj, proj, proj, cos, ssin, qn.reshape(1, -1), kvn.reshape(1, -1), wuq, wk)


def _attn_prompt_kernel(q_ref, k_ref, gate_ref, wv_ref, o_ref, m_ref, l_ref, acc_ref, *, tq, tk, nk):
    i, j = pl.program_id(1), pl.program_id(2)
    rows = A_HEADS * tq

    @pl.when(j == 0)
    def _():
        m_ref[...] = jnp.full(m_ref.shape, NEG, F32)
        l_ref[...] = jnp.zeros(l_ref.shape, F32)
        acc_ref[...] = jnp.zeros(acc_ref.shape, F32)

    @pl.when(j * tk <= i * tq + tq - 1)
    def _():
        q = q_ref[...].reshape(rows, A_QK)
        k = k_ref[...]
        s = _dot_nt(q, k)
        qpos = i * tq + lax.broadcasted_iota(jnp.int32, (rows, tk), 0) % tq
        kpos = j * tk + lax.broadcasted_iota(jnp.int32, (rows, tk), 1)
        s = jnp.where(kpos <= qpos, s, NEG)
        m_old = m_ref[...]
        m_new = jnp.maximum(m_old, jnp.max(s, axis=-1, keepdims=True))
        p = jnp.exp(s - m_new)
        alpha = jnp.exp(m_old - m_new)
        l_ref[...] = alpha * l_ref[...] + jnp.sum(p, axis=-1, keepdims=True)
        acc_ref[...] = alpha * acc_ref[...] + _dot(p.astype(BF16), k[:, 0:A_KV_LORA])
        m_ref[...] = m_new

    @pl.when(j == nk - 1)
    def _():
        for h in range(A_HEADS):
            sl = slice(h * tq, (h + 1) * tq)
            o_lat = (acc_ref[sl, :] / l_ref[sl, :]).astype(BF16)
            cols = slice(h * A_VDIM, (h + 1) * A_VDIM)
            o_ref[:, cols] = (_dot(o_lat, wv_ref[h]) * _silu(gate_ref[:, cols])).astype(BF16)


def _attn_prompt_call(qcat, kcat, proj, wv, bsz, seq, tq, tk):
    nq, nk = seq // tq, seq // tk
    kern = functools.partial(_attn_prompt_kernel, tq=tq, tk=tk, nk=nk)
    rows = A_HEADS * tq
    return pl.pallas_call(
        kern,
        grid=(bsz, nq, nk),
        in_specs=[
            pl.BlockSpec((A_HEADS, tq, A_QK), lambda b, i, j: (0, b * nq + i, 0)),
            pl.BlockSpec((tk, A_QK), lambda b, i, j: (b * nk + jnp.minimum(j, (i * tq + tq - 1) // tk), 0)),
            pl.BlockSpec((tq, A_WIDTH), lambda b, i, j: (b * nq + i, 0)),
            pl.BlockSpec(wv.shape, lambda b, i, j: (0, 0, 0)),
        ],
        out_specs=pl.BlockSpec((tq, A_WIDTH), lambda b, i, j: (b * nq + i, 0)),
        out_shape=jax.ShapeDtypeStruct((bsz * seq, A_WIDTH), BF16),
        scratch_shapes=[pltpu.VMEM((rows, 1), F32), pltpu.VMEM((rows, 1), F32), pltpu.VMEM((rows, A_KV_LORA), F32)],
        compiler_params=_cparams(("parallel", "parallel", "arbitrary"), 56),
        name="mla_attn_prompt",
    )(qcat, kcat, proj, wv)


def _attn_sample_kernel(pt_ref, q_ref, knew_ref, *refs, pages, nsteps, lq):
    ckv_refs, kpe_refs = refs[:pages], refs[pages:2 * pages]
    o_ref, m_ref, l_ref, acc_ref = refs[2 * pages:]
    c = pl.program_id(1)
    rows = A_HEADS * lq

    @pl.when(c == 0)
    def _():
        m_ref[...] = jnp.full(m_ref.shape, NEG, F32)
        l_ref[...] = jnp.zeros(l_ref.shape, F32)
        acc_ref[...] = jnp.zeros(acc_ref.shape, F32)

    q = q_ref[0]
    ql, qp = q[:, 0:A_KV_LORA], q[:, A_KV_LORA:A_KV_LORA + A_ROPE]

    def update(s, vals):
        m_old = m_ref[...]
        m_new = jnp.maximum(m_old, jnp.max(s, axis=-1, keepdims=True))
        p = jnp.exp(s - m_new)
        alpha = jnp.exp(m_old - m_new)
        l_ref[...] = alpha * l_ref[...] + jnp.sum(p, axis=-1, keepdims=True)
        p = p.astype(BF16)
        acc = alpha * acc_ref[...]
        for n, v in enumerate(vals):
            acc = acc + _dot(p[:, n * PAGE_SIZE:(n + 1) * PAGE_SIZE], v)
        acc_ref[...] = acc
        m_ref[...] = m_new

    ks = [r[0].astype(BF16) for r in ckv_refs]
    s = jnp.concatenate(
        [_dot_nt(ql, ks[n]) + _dot_nt(qp, kpe_refs[n][0].astype(BF16)) for n in range(pages)], axis=1)
    update(s, ks)

    @pl.when(c == nsteps - 1)
    def _():
        kn = jnp.concatenate([knew_ref[0], jnp.zeros((PAGE_SIZE - SUBLANES, A_QK), F32)], axis=0).astype(BF16)
        s_new = _dot_nt(q, kn)
        t_q = lax.broadcasted_iota(jnp.int32, (rows, PAGE_SIZE), 0) % lq
        t_k = lax.broadcasted_iota(jnp.int32, (rows, PAGE_SIZE), 1)
        s_new = jnp.where(t_k <= t_q, s_new, NEG)
        update(s_new, [kn[:, 0:A_KV_LORA]])
        o_ref[0] = acc_ref[...] / l_ref[...]


def _attn_sample_call(page_table, q, knew, cache_ckv, cache_kpe, pages):
    nb, n_pages = page_table.shape
    rows = q.shape[1]
    lq = rows // A_HEADS
    nsteps = n_pages // pages
    kern = functools.partial(_attn_sample_kernel, pages=pages, nsteps=nsteps, lq=lq)

    def page_map(n):
        return lambda b, c, pt: (pt[b, c * pages + n], 0, 0)

    in_specs = [
        pl.BlockSpec((1, rows, A_QK), lambda b, c, pt: (b, 0, 0)),
        pl.BlockSpec((1, SUBLANES, A_QK), lambda b, c, pt: (b, 0, 0)),
    ]
    in_specs += [pl.BlockSpec((1, PAGE_SIZE, A_KV_LORA), page_map(n)) for n in range(pages)]
    in_specs += [pl.BlockSpec((1, PAGE_SIZE, A_ROPE), page_map(n)) for n in range(pages)]
    grid_spec = pltpu.PrefetchScalarGridSpec(
        num_scalar_prefetch=1,
        grid=(nb, nsteps),
        in_specs=in_specs,
        out_specs=pl.BlockSpec((1, rows, A_KV_LORA), lambda b, c, pt: (b, 0, 0)),
        scratch_shapes=[pltpu.VMEM((rows, 1), F32), pltpu.VMEM((rows, 1), F32), pltpu.VMEM((rows, A_KV_LORA), F32)],
    )
    return pl.pallas_call(
        kern,
        grid_spec=grid_spec,
        out_shape=jax.ShapeDtypeStruct((nb, rows, A_KV_LORA), F32),
        compiler_params=_cparams(("parallel", "arbitrary"), 48),
        name="mla_attn_sample",
    )(page_table, q, knew, *([cache_ckv] * pages), *([cache_kpe] * pages))


def _oproj_kernel(o_ref, wv_ref, gate_ref, out_ref):
    out_ref[...] = (_dot(o_ref[0].astype(BF16), wv_ref[0]) * _silu(gate_ref[...])).astype(BF16)


def _oproj_call(o_lat, wv, proj):
    t = o_lat.shape[1]
    return pl.pallas_call(
        _oproj_kernel,
        grid=(A_HEADS,),
        in_specs=[
            pl.BlockSpec((1, t, A_KV_LORA), lambda h: (h, 0, 0)),
            pl.BlockSpec((1, A_KV_LORA, A_VDIM), lambda h: (h, 0, 0)),
            pl.BlockSpec((t, A_VDIM), lambda h: (0, h)),
        ],
        out_specs=pl.BlockSpec((t, A_VDIM), lambda h: (0, h)),
        out_shape=jax.ShapeDtypeStruct((t, A_WIDTH), BF16),
        compiler_params=_cparams(("parallel",), 32),
        name="mla_oproj_sample",
    )(o_lat, wv, proj)


def _group_norm(y, nw):
    outs = []
    for g in range(B_GROUPS):
        sl = slice(g * B_GW, (g + 1) * B_GW)
        outs.append(_rms(y[:, sl], nw[:, sl]))
    return jnp.concatenate(outs, axis=1)


def _ssd_prompt_kernel(proj_ref, cw_ref, cb_ref, dtb_ref, alog_ref, dexp_ref, nw_ref, e_ref, tril_ref,
                       y_ref, tail_ref, hlast_ref, xp_ref, h_ref, yacc_ref, *, nchunks):
    c = pl.program_id(1)
    q = CHUNK

    @pl.when(c == 0)
    def _():
        xp_ref[0:SUBLANES, :] = jnp.zeros((SUBLANES, B_CONV_DIM), F32)
        h_ref[...] = jnp.zeros(h_ref.shape, F32)

    raw = proj_ref[:, B_WIDTH:B_WIDTH + B_CONV_DIM]
    xp_ref[SUBLANES:SUBLANES + q, :] = raw
    conv = cb_ref[...] + cw_ref[3:4, :] * raw
    for back in range(1, B_CONV):
        conv = conv + cw_ref[3 - back:4 - back, :] * xp_ref[SUBLANES - back:SUBLANES - back + q, :]
    xp_ref[0:SUBLANES, :] = raw[q - SUBLANES:q, :]
    tail_ref[0] = raw[q - SUBLANES:q, :]
    xc = _silu(conv)
    x = xc[:, 0:B_WIDTH]
    bm = xc[:, B_WIDTH:B_WIDTH + B_GN].astype(BF16)
    cm = xc[:, B_WIDTH + B_GN:B_CONV_DIM].astype(BF16)

    dt = _softplus(proj_ref[:, B_WIDTH + B_CONV_DIM:B_IN_PAD] + dtb_ref[...])
    dta = dt * (-jnp.exp(alog_ref[...]))
    acum = _split_dot(dta, tril_ref[...], 3, left=True)
    acum_t = acum.T
    alast = acum[q - 1:q, :]
    e = e_ref[...]
    dtx = _split_dot(dt, e, 2)
    eax = _split_dot(jnp.exp(acum), e, 2)
    wx = _split_dot(jnp.exp(alast - acum) * dt, e, 2)
    lane = lax.broadcasted_iota(jnp.int32, (q, B_WIDTH), 1)
    even = (lane % LANES) < B_HEADDIM
    xdt = x * dtx
    xdt_even = jnp.where(even, xdt, 0.0).astype(BF16)
    xdt_odd = jnp.where(even, 0.0, xdt).astype(BF16)
    xw = (x * wx).astype(BF16)
    causal = lax.broadcasted_iota(jnp.int32, (q, q), 0) >= lax.broadcasted_iota(jnp.int32, (q, q), 1)
    dec_col = jnp.exp(acum_t[:, q - 1:q])

    for g in range(B_GROUPS):
        cg = cm[:, g * B_STATE:(g + 1) * B_STATE]
        bg = bm[:, g * B_STATE:(g + 1) * B_STATE]
        cb = _dot_nt(cg, bg)
        hg = h_ref[g * B_HPG:(g + 1) * B_HPG].reshape(B_GW, B_STATE)
        ystate = _dot_nt(cg, hg.astype(BF16))
        for pr in range(B_HPG // 2):
            ls = []
            for hh in (2 * pr, 2 * pr + 1):
                h = g * B_HPG + hh
                diff = acum[:, h:h + 1] - acum_t[h:h + 1, :]
                ls.append((cb * jnp.exp(jnp.where(causal, diff, NEG))).astype(BF16))
            col = g * B_GW + pr * LANES
            lhs = jnp.concatenate(ls, axis=1)
            rhs = jnp.concatenate([xdt_even[:, col:col + LANES], xdt_odd[:, col:col + LANES]], axis=0)
            yacc_ref[:, col:col + LANES] = _dot(lhs, rhs) + ystate[:, pr * LANES:(pr + 1) * LANES] * eax[:, col:col + LANES]
        snew = _dot_tn(xw[:, g * B_GW:(g + 1) * B_GW], bg)
        for hh in range(B_HPG):
            h = g * B_HPG + hh
            dec = jnp.broadcast_to(dec_col[h:h + 1, :], (B_HEADDIM, B_STATE))
            h_ref[h] = h_ref[h] * dec + snew[hh * B_HEADDIM:(hh + 1) * B_HEADDIM, :]

    y = (yacc_ref[...] + dexp_ref[...] * x) * _silu(proj_ref[:, 0:B_WIDTH])
    y_ref[...] = _group_norm(y, nw_ref[...]).astype(BF16)

    @pl.when(c == nchunks - 1)
    def _():
        hlast_ref[0] = h_ref[...]


def _ssd_prompt_call(proj, cw, cb, dtb, alog, dexp, nw, e, tril, bsz, seq):
    nchunks = seq // CHUNK
    kern = functools.partial(_ssd_prompt_kernel, nchunks=nchunks)
    const = lambda b, c: (0, 0)
    return pl.pallas_call(
        kern,
        grid=(bsz, nchunks),
        in_specs=[
            pl.BlockSpec((CHUNK, B_IN_PAD), lambda b, c: (b * nchunks + c, 0)),
            pl.BlockSpec((B_CONV, B_CONV_DIM), const),
            pl.BlockSpec((1, B_CONV_DIM), const),
            pl.BlockSpec((1, LANES), const),
            pl.BlockSpec((1, LANES), const),
            pl.BlockSpec((1, B_WIDTH), const),
            pl.BlockSpec((1, B_WIDTH), const),
            pl.BlockSpec((LANES, B_WIDTH), const),
            pl.BlockSpec((CHUNK, CHUNK), const),
        ],
        out_specs=[
            pl.BlockSpec((CHUNK, B_WIDTH), lambda b, c: (b * nchunks + c, 0)),
            pl.BlockSpec((1, SUBLANES, B_CONV_DIM), lambda b, c: (b, 0, 0)),
            pl.BlockSpec((1, B_HEADS, B_HEADDIM, B_STATE), lambda b, c: (b, 0, 0, 0)),
        ],
        out_shape=[
            jax.ShapeDtypeStruct((bsz * seq, B_WIDTH), BF16),
            jax.ShapeDtypeStruct((bsz, SUBLANES, B_CONV_DIM), F32),
            jax.ShapeDtypeStruct((bsz, B_HEADS, B_HEADDIM, B_STATE), F32),
        ],
        scratch_shapes=[
            pltpu.VMEM((SUBLANES + CHUNK, B_CONV_DIM), F32),
            pltpu.VMEM((B_HEADS, B_HEADDIM, B_STATE), F32),
            pltpu.VMEM((CHUNK, B_WIDTH), F32),
        ],
        compiler_params=_cparams(("parallel", "arbitrary"), 56),
        name="ssd_prompt",
    )(proj, cw, cb, dtb, alog, dexp, nw, e, tril)


def _ssd_sample_conv_kernel(x_ref, prev_ref, cw_ref, cb_ref, o_ref, *, lq):
    rows = [prev_ref[n] for n in range(B_CONV - 1)] + [x_ref[n] for n in range(lq)]
    for t in range(lq):
        conv = cb_ref[...]
        for j in range(B_CONV):
            conv = conv + cw_ref[j:j + 1, :] * rows[t + j]
        o_ref[t] = _silu(conv)


def _ssd_sample_conv_call(proj_t, prev_t, cw, cb):
    lq, nb = proj_t.shape[:2]
    tc = 2048
    off = B_WIDTH // tc
    kern = functools.partial(_ssd_sample_conv_kernel, lq=lq)
    return pl.pallas_call(
        kern,
        grid=(B_CONV_DIM // tc,),
        in_specs=[
            pl.BlockSpec((lq, nb, tc), lambda j: (0, 0, off + j)),
            pl.BlockSpec((B_CONV - 1, nb, tc), lambda j: (0, 0, j)),
            pl.BlockSpec((B_CONV, tc), lambda j: (0, j)),
            pl.BlockSpec((1, tc), lambda j: (0, j)),
        ],
        out_specs=pl.BlockSpec((lq, nb, tc), lambda j: (0, 0, j)),
        out_shape=jax.ShapeDtypeStruct((lq, nb, B_CONV_DIM), F32),
        compiler_params=_cparams(("parallel",), 40),
        name="ssd_sample_conv",
    )(proj_t, prev_t, cw, cb)


def _ssd_sample_intra_kernel(xc_ref, dt_ref, dtb_ref, alog_ref, dexp_ref, r_ref, e_ref,
                             yin_ref, eax_ref, xw_ref, dec_ref, *, lq):
    qi = pl.program_id(0)
    a = -jnp.exp(alog_ref[...])
    dts = [_softplus(dt_ref[k] + dtb_ref[...]) for k in range(lq)]
    acums = []
    run = None
    for k in range(lq):
        run = dts[k] * a if run is None else run + dts[k] * a
        acums.append(run)
    acum_q = acums[0]
    for k in range(1, lq):
        acum_q = jnp.where(qi >= k, acums[k], acum_q)
    e, r = e_ref[...], r_ref[...]
    x_q = xc_ref[qi, :, 0:B_WIDTH]
    c_q = xc_ref[qi, :, B_WIDTH + B_GN:B_CONV_DIM]
    y = dexp_ref[...] * x_q
    for k in range(lq):
        b_k = xc_ref[k, :, B_WIDTH:B_WIDTH + B_GN]
        cb = _split_dot(c_q * b_k, r, 2)
        m = cb * jnp.exp(jnp.where(qi >= k, acum_q - acums[k], NEG)) * dts[k]
        y = y + _split_dot(m, e, 2) * xc_ref[k, :, 0:B_WIDTH]
    yin_ref[0] = y
    eax_ref[0] = _split_dot(jnp.exp(acum_q), e, 2)
    dt_q = dts[0]
    for k in range(1, lq):
        dt_q = jnp.where(qi == k, dts[k], dt_q)
    xw_ref[0] = x_q * _split_dot(jnp.exp(acums[lq - 1] - acum_q) * dt_q, e, 2)
    dec_ref[...] = jnp.exp(acums[lq - 1])


def _ssd_sample_intra_call(xc, proj_t, dtb, alog, dexp, r, e):
    lq, nb = xc.shape[:2]
    kern = functools.partial(_ssd_sample_intra_kernel, lq=lq)
    tok = lambda q: (q, 0, 0)
    const = lambda q: (0, 0)
    big = jax.ShapeDtypeStruct((lq, nb, B_WIDTH), F32)
    return pl.pallas_call(
        kern,
        grid=(lq,),
        in_specs=[
            pl.BlockSpec((lq, nb, B_CONV_DIM), lambda q: (0, 0, 0)),
            pl.BlockSpec((lq, nb, LANES), lambda q: (0, 0, (B_WIDTH + B_CONV_DIM) // LANES)),
            pl.BlockSpec((1, LANES), const),
            pl.BlockSpec((1, LANES), const),
            pl.BlockSpec((1, B_WIDTH), const),
            pl.BlockSpec((B_GN, LANES), const),
            pl.BlockSpec((LANES, B_WIDTH), const),
        ],
        out_specs=[
            pl.BlockSpec((1, nb, B_WIDTH), tok),
            pl.BlockSpec((1, nb, B_WIDTH), tok),
            pl.BlockSpec((1, nb, B_WIDTH), tok),
            pl.BlockSpec((nb, LANES), const),
        ],
        out_shape=[big, big, big, jax.ShapeDtypeStruct((nb, LANES), F32)],
        compiler_params=_cparams(("arbitrary",), 56),
        name="ssd_sample_intra",
    )(xc, proj_t, dtb, alog, dexp, r, e)


def _ssd_sample_state_kernel(dec_ref, h0_ref, c_ref, b_ref, xw_ref, yin_ref, eax_ref, z_ref, nw_ref,
                             y_ref, hl_ref, c8, b8, xw8, *, lq):
    s = pl.program_id(0)
    for buf in (c8, b8, xw8):
        buf[lq:SUBLANES, :] = jnp.zeros((SUBLANES - lq, buf.shape[1]), F32)
    for t in range(lq):
        c8[t:t + 1, :] = c_ref[t, 0]
        b8[t:t + 1, :] = b_ref[t, 0]
        xw8[t:t + 1, :] = xw_ref[t, 0]
    ys = []
    for g in range(B_GROUPS):
        hg = h0_ref[0, g * B_HPG:(g + 1) * B_HPG].reshape(B_GW, B_STATE).astype(BF16)
        cg = c8[:, g * B_STATE:(g + 1) * B_STATE].astype(BF16)
        bg = b8[:, g * B_STATE:(g + 1) * B_STATE].astype(BF16)
        ys.append(_dot_nt(cg, hg))
        snew = _dot_tn(xw8[:, g * B_GW:(g + 1) * B_GW].astype(BF16), bg)
        for hh in range(B_HPG):
            h = g * B_HPG + hh
            hl_ref[0, h] = h0_ref[0, h] * dec_ref[s, h] + snew[hh * B_HEADDIM:(hh + 1) * B_HEADDIM, :]
    ystate = jnp.concatenate(ys, axis=1)
    for t in range(lq):
        y = (yin_ref[t, 0] + ystate[t:t + 1, :] * eax_ref[t, 0]) * _silu(z_ref[t, 0])
        y_ref[0, t:t + 1, :] = _group_norm(y, nw_ref[...])


def _ssd_sample_state_call(dec, h0, xc, xw, yin, eax, proj_t, nw):
    lq, nb = xc.shape[:2]
    kern = functools.partial(_ssd_sample_state_kernel, lq=lq)
    four = lambda a: a.reshape(lq, nb, 1, a.shape[-1])
    row = lambda blk: (lambda s, d: (0, s, 0, blk))
    grid_spec = pltpu.PrefetchScalarGridSpec(
        num_scalar_prefetch=1,
        grid=(nb,),
        in_specs=[
            pl.BlockSpec((1, B_HEADS, B_HEADDIM, B_STATE), lambda s, d: (s, 0, 0, 0)),
            pl.BlockSpec((lq, 1, 1, B_GN), row((B_WIDTH + B_GN) // B_GN)),
            pl.BlockSpec((lq, 1, 1, B_GN), row(B_WIDTH // B_GN)),
            pl.BlockSpec((lq, 1, 1, B_WIDTH), row(0)),
            pl.BlockSpec((lq, 1, 1, B_WIDTH), row(0)),
            pl.BlockSpec((lq, 1, 1, B_WIDTH), row(0)),
            pl.BlockSpec((lq, 1, 1, B_WIDTH), row(0)),
            pl.BlockSpec((1, B_WIDTH), lambda s, d: (0, 0)),
        ],
        out_specs=[
            pl.BlockSpec((1, lq, B_WIDTH), lambda s, d: (s, 0, 0)),
            pl.BlockSpec((1, B_HEADS, B_HEADDIM, B_STATE), lambda s, d: (s, 0, 0, 0)),
        ],
        scratch_shapes=[pltpu.VMEM((SUBLANES, B_GN), F32), pltpu.VMEM((SUBLANES, B_GN), F32),
                        pltpu.VMEM((SUBLANES, B_WIDTH), F32)],
    )
    return pl.pallas_call(
        kern,
        grid_spec=grid_spec,
        out_shape=[jax.ShapeDtypeStruct((nb, lq, B_WIDTH), F32),
                   jax.ShapeDtypeStruct((nb, B_HEADS, B_HEADDIM, B_STATE), F32)],
        compiler_params=_cparams(("arbitrary",), 40),
        name="ssd_sample_state",
    )(dec, h0, four(xc), four(xc), four(xw), four(yin), four(eax), four(proj_t), nw)


def _gmlp_kernel(p_ref, wm_ref, bias_ref, lng_ref, lnb_ref, o_ref, *v_refs):
    q = CHUNK
    u = _gelu(p_ref[:, 0:C_WIDTH])
    vp = _gelu(p_ref[:, C_WIDTH:2 * C_WIDTH])
    mu = jnp.mean(vp, axis=-1, keepdims=True)
    var = jnp.mean(jnp.square(vp - mu), axis=-1, keepdims=True)
    v = (vp - mu) * lax.rsqrt(var + EPS) * lng_ref[...] + lnb_ref[...]
    if v_refs:
        v_refs[0][...] = v
    vb = v.astype(BF16)
    causal = lax.broadcasted_iota(jnp.int32, (q, q), 0) >= lax.broadcasted_iota(jnp.int32, (q, q), 1)
    for g in range(C_GROUPS):
        sl = slice(g * C_GDIM, (g + 1) * C_GDIM)
        w = jnp.where(causal, wm_ref[g], 0.0).astype(BF16)
        mix = _dot(w, vb[:, sl]) + bias_ref[:, sl]
        gate = p_ref[:, 2 * C_WIDTH + g * C_GDIM:2 * C_WIDTH + (g + 1) * C_GDIM]
        o_ref[:, sl] = (u[:, sl] * mix * _silu(gate)).astype(BF16)


def _gmlp_call(proj, wm, bias, lng, lnb, want_v):
    t = proj.shape[0]
    row = lambda i: (i, 0)
    const = lambda i: (0, 0)
    out_specs = [pl.BlockSpec((CHUNK, C_WIDTH), row)]
    out_shape = [jax.ShapeDtypeStruct((t, C_WIDTH), BF16)]
    if want_v:
        out_specs.append(pl.BlockSpec((CHUNK, C_WIDTH), row))
        out_shape.append(jax.ShapeDtypeStruct((t, C_WIDTH), F32))
    return pl.pallas_call(
        _gmlp_kernel,
        grid=(t // CHUNK,),
        in_specs=[
            pl.BlockSpec((CHUNK, 3 * C_WIDTH), row),
            pl.BlockSpec((C_GROUPS, CHUNK, CHUNK), lambda i: (0, 0, 0)),
            pl.BlockSpec((CHUNK, C_WIDTH), const),
            pl.BlockSpec((1, C_WIDTH), const),
            pl.BlockSpec((1, C_WIDTH), const),
        ],
        out_specs=out_specs,
        out_shape=out_shape,
        compiler_params=_cparams(("parallel",), 48),
        name="gmlp_mix",
    )(proj, wm, bias, lng.reshape(1, C_WIDTH), lnb.reshape(1, C_WIDTH))


def _pad_cols(w, n):
    return jnp.pad(w, ((0, 0), (0, n - w.shape[1])))


def _rope_tables(pos):
    half = A_ROPE // 2
    inv = ROPE_THETA ** (-jnp.arange(half, dtype=F32) / half)
    ang = pos.astype(F32)[:, None] * inv[None, :]
    cos, sin = jnp.cos(ang), jnp.sin(ang)
    return jnp.tile(cos, (1, LANES // half)), jnp.tile(jnp.concatenate([-sin, sin], axis=1), (1, LANES // A_ROPE))


def _head_expand():
    rows = lax.broadcasted_iota(jnp.int32, (LANES, B_WIDTH), 0)
    cols = lax.broadcasted_iota(jnp.int32, (LANES, B_WIDTH), 1)
    return (cols // B_HEADDIM == rows).astype(BF16)


def _group_reduce():
    rows = lax.broadcasted_iota(jnp.int32, (B_GN, LANES), 0)
    cols = lax.broadcasted_iota(jnp.int32, (B_GN, LANES), 1)
    return ((rows // B_STATE == cols // B_HPG) & (cols < B_HEADS)).astype(BF16)


def kernel(x_prompt, x_sample, c_prompt, c_sample, cache_ckv, cache_kpe, page_table, state_ssm, state_conv, ada_w, ada_b, norm_w, norm_f, a_w_in, a_q_norm, a_w_uq, a_kv_norm, a_w_ukv, a_w_out, b_w_in, b_conv_w, b_conv_b, b_dt_bias, b_a_log, b_d, b_norm, b_w_out, c_w_in, c_ln_g, c_ln_b, c_ws, c_bs, c_w_out):
    bsz, seq, _ = x_prompt.shape
    nb, lq, _ = x_sample.shape
    tp, ts = bsz * seq, nb * lq
    past_len = page_table.shape[1] * PAGE_SIZE

    mods = _ada_call(jnp.concatenate([c_prompt, c_sample], axis=0), ada_w, ada_b)
    e_mat, r_mat = _head_expand(), _group_reduce()
    tril = (lax.broadcasted_iota(jnp.int32, (CHUNK, CHUNK), 0)
            >= lax.broadcasted_iota(jnp.int32, (CHUNK, CHUNK), 1)).astype(BF16)
    cos_p, sin_p = _rope_tables(jnp.tile(jnp.arange(seq, dtype=jnp.int32), bsz))
    cos_s, sin_s = _rope_tables(jnp.tile(past_len + jnp.arange(lq, dtype=jnp.int32), nb))

    xp = x_prompt.reshape(tp, D_MODEL)
    xs = x_sample.reshape(ts, D_MODEL)
    tm_p, tm_s = 1024, ts
    outs = dict(ckv_p=[], kpe_p=[], ckv_s=[], kpe_s=[], ssm_p=[], conv_p=[], ssm_s=[], conv_s=[], v_s=[])

    for l in range(DEPTH):
        kind, j = l % 3, l // 3
        mods_p = mods[l, :bsz].reshape(bsz, 1, 3 * D_MODEL)
        mods_s = jnp.repeat(mods[l, bsz:], lq, axis=0).reshape(1, ts, 3 * D_MODEL)
        if kind == 0:
            o1, o2, o3 = A_Q_LORA, A_Q_LORA + A_KV_LORA, A_Q_LORA + A_KV_LORA + A_ROPE
            w = a_w_in[j]
            w_in = _pad_cols(jnp.concatenate([w[:, o3:], w[:, :o3]], axis=1), A_IN_PAD).astype(BF16)
            wuq = a_w_uq[j].reshape(A_Q_LORA, A_HEADS, A_NOPE + A_ROPE)
            wuq = jnp.concatenate([wuq[:, :, :A_NOPE].reshape(A_Q_LORA, -1),
                                   wuq[:, :, A_NOPE:].reshape(A_Q_LORA, -1)], axis=1).astype(BF16)
            wk = a_w_ukv[j][:, :, :A_NOPE].transpose(1, 2, 0).astype(BF16)
            wv = a_w_ukv[j][:, :, A_NOPE:].transpose(1, 0, 2).astype(BF16)
            w_out = a_w_out[j].astype(BF16)

            proj_p = _in_call(xp, norm_w[l], mods_p, w_in, tm_p, 640, seq)
            qcat, kcat, ckv, kpe = _qkv_call(proj_p, cos_p, sin_p, a_q_norm[j], a_kv_norm[j], wuq, wk, 256)
            a_p = _attn_prompt_call(qcat, kcat, proj_p, wv, bsz, seq, 128, 512)
            xp = _out_call(a_p, w_out, xp, mods_p, tm_p, 512, seq)
            outs["ckv_p"].append(ckv.reshape(bsz, seq, A_KV_LORA))
            outs["kpe_p"].append(kpe[:, :A_ROPE].reshape(bsz, seq, A_ROPE))

            proj_s = _in_call(xs, norm_w[l], mods_s, w_in, tm_s, 640, ts)
            qcat, kcat, ckv, kpe = _qkv_call(proj_s, cos_s, sin_s, a_q_norm[j], a_kv_norm[j], wuq, wk, 256)
            q_s = qcat.reshape(A_HEADS, nb, lq, A_QK).transpose(1, 0, 2, 3).reshape(nb, A_HEADS * lq, A_QK)
            knew = jnp.concatenate(
                [ckv, kpe], axis=1).reshape(nb, lq, A_QK)
            knew = jnp.pad(knew, ((0, 0), (0, SUBLANES - lq), (0, 0)))
            o_lat = _attn_sample_call(page_table, q_s, knew, cache_ckv[j], cache_kpe[j], 16)
            o_lat = o_lat.reshape(nb, A_HEADS, lq, A_KV_LORA).transpose(1, 0, 2, 3).reshape(A_HEADS, ts, A_KV_LORA)
            a_s = _oproj_call(o_lat, wv, proj_s)
            xs = _out_call(a_s, w_out, xs, mods_s, tm_s, 512, ts)
            outs["ckv_s"].append(ckv.reshape(nb, lq, A_KV_LORA))
            outs["kpe_s"].append(kpe[:, :A_ROPE].reshape(nb, lq, A_ROPE))
        elif kind == 1:
            w_in = _pad_cols(b_w_in[j], B_IN_PAD).astype(BF16)
            w_out = b_w_out[j].astype(BF16)
            cw, cb = b_conv_w[j], b_conv_b[j].reshape(1, B_CONV_DIM)
            dtb = jnp.pad(b_dt_bias[j], (0, LANES - B_HEADS)).reshape(1, LANES)
            alog = jnp.pad(b_a_log[j], (0, LANES - B_HEADS)).reshape(1, LANES)
            dexp = jnp.repeat(b_d[j], B_HEADDIM).reshape(1, B_WIDTH)
            nw = b_norm[j].reshape(1, B_WIDTH)

            proj_p = _in_call(xp, norm_w[l], mods_p, w_in, tm_p, 1152, seq)
            y_p, tail, h_last = _ssd_prompt_call(proj_p, cw, cb, dtb, alog, dexp, nw, e_mat, tril, bsz, seq)
            xp = _out_call(y_p, w_out, xp, mods_p, tm_p, 512, seq)
            outs["ssm_p"].append(h_last)
            outs["conv_p"].append(tail[:, SUBLANES - (B_CONV - 1):])

            proj_s = _in_call(xs, norm_w[l], mods_s, w_in, tm_s, 1152, ts)
            proj_t = proj_s.reshape(nb, lq, B_IN_PAD).transpose(1, 0, 2)
            prev_t = state_conv[j].transpose(1, 0, 2)
            xc = _ssd_sample_conv_call(proj_t, prev_t, cw, cb)
            yin, eax, xw, dec = _ssd_sample_intra_call(xc, proj_t, dtb, alog, dexp, r_mat, e_mat)
            y_s, h_last = _ssd_sample_state_call(dec, state_ssm[j], xc, xw, yin, eax, proj_t, nw)
            xs = _out_call(y_s.reshape(ts, B_WIDTH), w_out, xs, mods_s, tm_s, 512, ts)
            raw = jnp.concatenate([state_conv[j], proj_s[:, B_WIDTH:B_WIDTH + B_CONV_DIM].reshape(nb, lq, B_CONV_DIM)], axis=1)
            outs["ssm_s"].append(h_last)
            outs["conv_s"].append(raw[:, -(B_CONV - 1):])
        else:
            w_in = c_w_in[j].astype(BF16)
            w_out = c_w_out[j].astype(BF16)
            bias_p = jnp.repeat(c_bs[j].T, C_GDIM, axis=1)
            proj_p = _in_call(xp, norm_w[l], mods_p, w_in, tm_p, 1024, seq)
            (m_p,) = _gmlp_call(proj_p, c_ws[j], bias_p, c_ln_g[j], c_ln_b[j], False)
            xp = _out_call(m_p, w_out, xp, mods_p, tm_p, 512, seq)

            reps = CHUNK // lq
            wm_s = jnp.einsum("ab,gts->gatbs", jnp.eye(reps, dtype=F32), c_ws[j][:, :lq, :lq]).reshape(C_GROUPS, CHUNK, CHUNK)
            bias_s = jnp.tile(jnp.repeat(c_bs[j][:, :lq].T, C_GDIM, axis=1), (reps, 1))
            proj_s = _in_call(xs, norm_w[l], mods_s, w_in, tm_s, 1024, ts)
            m_s, v_s = _gmlp_call(proj_s, wm_s, bias_s, c_ln_g[j], c_ln_b[j], True)
            xs = _out_call(m_s, w_out, xs, mods_s, tm_s, 512, ts)
            outs["v_s"].append(v_s.reshape(nb, lq, C_WIDTH))

    y_p = _norm_call(xp, norm_f, 1024).reshape(bsz, seq, D_MODEL)
    y_s = _norm_call(xs, norm_f, ts).reshape(nb, lq, D_MODEL)
    return (y_p, y_s,
            jnp.stack(outs["ckv_p"]), jnp.stack(outs["kpe_p"]), jnp.stack(outs["ckv_s"]), jnp.stack(outs["kpe_s"]),
            jnp.stack(outs["ssm_p"]), jnp.stack(outs["conv_p"]), jnp.stack(outs["ssm_s"]), jnp.stack(outs["conv_s"]),
            jnp.stack(outs["v_s"]))
```

```python
import functools
import math

import jax
import jax.numpy as jnp
from jax import lax
from jax.experimental import pallas as pl
from jax.experimental.pallas import tpu as pltpu

F32 = jnp.float32
BF16 = jnp.bfloat16

D_MODEL = 2048
DEPTH = 4
EPS = 1e-6
A_HEADS = 16
A_NOPE = 128
A_ROPE = 64
A_VDIM = 128
A_Q_LORA = 512
A_KV_LORA = 512
A_WIDTH = A_HEADS * A_VDIM
A_SCALE = (A_NOPE + A_ROPE) ** -0.5
ROPE_THETA = 10000.0
PAGE_SIZE = 128
B_WIDTH = 2 * D_MODEL
B_HEADDIM = 64
B_HEADS = B_WIDTH // B_HEADDIM
B_GROUPS = 8
B_STATE = 128
B_CONV = 4
B_GN = B_GROUPS * B_STATE
B_CONV_DIM = B_WIDTH + 2 * B_GN
B_HPG = B_HEADS // B_GROUPS
B_GW = B_WIDTH // B_GROUPS
CHUNK = 128
C_WIDTH = 2 * D_MODEL
C_GROUPS = 16
C_GDIM = C_WIDTH // C_GROUPS

LANES = 128
SUBLANES = 8
VMEM_BYTES = 64 * 1024 * 1024

A_IN_PAD = 3200
A_QK = A_KV_LORA + LANES
B_IN_PAD = B_WIDTH + B_CONV_DIM + LANES
NEG = -1e30

NT_DIMS = (((1,), (1,)), ((), ()))
TN_DIMS = (((0,), (0,)), ((), ()))


def _cparams(sem, vmem_mb):
    return pltpu.CompilerParams(dimension_semantics=sem, vmem_limit_bytes=vmem_mb * 1024 * 1024)


def _silu(x):
    return x * (1.0 / (1.0 + jnp.exp(-x)))


def _gelu(x):
    return 0.5 * x * (1.0 + jnp.tanh(math.sqrt(2.0 / math.pi) * (x + 0.044715 * (x * x * x))))


def _softplus(x):
    return jnp.maximum(x, 0.0) + jnp.log1p(jnp.exp(-jnp.abs(x)))


def _rms(x, w):
    return x * lax.rsqrt(jnp.mean(x * x, axis=-1, keepdims=True) + EPS) * w


def _dot(a, b):
    return jnp.dot(a, b, preferred_element_type=F32)


def _dot_nt(a, b):
    return lax.dot_general(a, b, NT_DIMS, preferred_element_type=F32)


def _dot_tn(a, b):
    return lax.dot_general(a, b, TN_DIMS, preferred_element_type=F32)


def _split_dot(a, sel, parts, left=False):
    acc = None
    rem = a
    for _ in range(parts):
        piece = rem.astype(BF16)
        rem = rem - piece.astype(F32)
        term = _dot(sel, piece) if left else _dot(piece, sel)
        acc = term if acc is None else acc + term
    return acc


def _ada_kernel(c_ref, w_ref, b_ref, o_ref):
    s = _silu(c_ref[...]).astype(BF16)
    o_ref[0] = _dot(s, w_ref[0].astype(BF16)) + b_ref[0]


def _ada_call(c_all, ada_w, ada_b):
    n_rows = c_all.shape[0]
    tn = 1024
    return pl.pallas_call(
        _ada_kernel,
        grid=(DEPTH, 3 * D_MODEL // tn),
        in_specs=[
            pl.BlockSpec((n_rows, D_MODEL), lambda l, j: (0, 0)),
            pl.BlockSpec((1, D_MODEL, tn), lambda l, j: (l, 0, j)),
            pl.BlockSpec((1, 1, tn), lambda l, j: (l, 0, j)),
        ],
        out_specs=pl.BlockSpec((1, n_rows, tn), lambda l, j: (l, 0, j)),
        out_shape=jax.ShapeDtypeStruct((DEPTH, n_rows, 3 * D_MODEL), F32),
        compiler_params=_cparams(("parallel", "parallel"), 40),
        name="ada_mod",
    )(c_all, ada_w, ada_b.reshape(DEPTH, 1, 3 * D_MODEL))


def _in_kernel(x_ref, nw_ref, sh_ref, sc_ref, w_ref, o_ref, h_ref):
    @pl.when(pl.program_id(1) == 0)
    def _():
        y = _rms(x_ref[...], nw_ref[...])
        h_ref[...] = (y * (1.0 + sc_ref[0]) + sh_ref[0]).astype(BF16)

    o_ref[...] = _dot(h_ref[...], w_ref[...])


def _in_call(x, nw, mods, w, tm, tn, rows_per_mod):
    t, n = x.shape[0], w.shape[1]
    r = mods.shape[1]
    return pl.pallas_call(
        _in_kernel,
        grid=(t // tm, n // tn),
        in_specs=[
            pl.BlockSpec((tm, D_MODEL), lambda i, j: (i, 0)),
            pl.BlockSpec((1, D_MODEL), lambda i, j: (0, 0)),
            pl.BlockSpec((1, r, D_MODEL), lambda i, j: (i * tm // rows_per_mod, 0, 0)),
            pl.BlockSpec((1, r, D_MODEL), lambda i, j: (i * tm // rows_per_mod, 0, 1)),
            pl.BlockSpec((D_MODEL, tn), lambda i, j: (0, j)),
        ],
        out_specs=pl.BlockSpec((tm, tn), lambda i, j: (i, j)),
        out_shape=jax.ShapeDtypeStruct((t, n), F32),
        scratch_shapes=[pltpu.VMEM((tm, D_MODEL), BF16)],
        compiler_params=_cparams(("parallel", "arbitrary"), 56),
        name="in_proj",
    )(x, nw.reshape(1, D_MODEL), mods, mods, w)


def _out_kernel(a_ref, w_ref, x_ref, g_ref, o_ref):
    o_ref[...] = x_ref[...] + g_ref[0] * _dot(a_ref[...].astype(BF16), w_ref[...])


def _out_call(a, w, x, mods, tm, tn, rows_per_mod):
    t, k = a.shape
    r = mods.shape[1]
    nj = D_MODEL // tn
    return pl.pallas_call(
        _out_kernel,
        grid=(t // tm, nj),
        in_specs=[
            pl.BlockSpec((tm, k), lambda i, j: (i, 0)),
            pl.BlockSpec((k, tn), lambda i, j: (0, j)),
            pl.BlockSpec((tm, tn), lambda i, j: (i, j)),
            pl.BlockSpec((1, r, tn), lambda i, j: (i * tm // rows_per_mod, 0, 2 * nj + j)),
        ],
        out_specs=pl.BlockSpec((tm, tn), lambda i, j: (i, j)),
        out_shape=jax.ShapeDtypeStruct((t, D_MODEL), F32),
        compiler_params=_cparams(("parallel", "parallel"), 56),
        name="out_proj",
    )(a, w, x, mods)


def _norm_kernel(x_ref, w_ref, o_ref):
    o_ref[...] = _rms(x_ref[...], w_ref[...])


def _norm_call(x, w, tm):
    t = x.shape[0]
    return pl.pallas_call(
        _norm_kernel,
        grid=(t // tm,),
        in_specs=[pl.BlockSpec((tm, D_MODEL), lambda i: (i, 0)), pl.BlockSpec((1, D_MODEL), lambda i: (0, 0))],
        out_specs=pl.BlockSpec((tm, D_MODEL), lambda i: (i, 0)),
        out_shape=jax.ShapeDtypeStruct((t, D_MODEL), F32),
        compiler_params=_cparams(("parallel",), 40),
        name="final_norm",
    )(x, w.reshape(1, D_MODEL))


def _rope(x, cos, ssin):
    lane = lax.broadcasted_iota(jnp.int32, x.shape, 1)
    first = (lane % A_ROPE) < (A_ROPE // 2)
    partner = jnp.where(first, pltpu.roll(x, LANES - A_ROPE // 2, 1), pltpu.roll(x, A_ROPE // 2, 1))
    return x * cos + partner * ssin


def _qkv_kernel(cq_ref, ckv_ref, kpe_ref, cos_ref, sin_ref, qn_ref, kvn_ref, wuq_ref, wk_ref,
                qcat_ref, kcat_ref, ckv_o, kpe_o):
    cos, ssin = cos_ref[...], sin_ref[...]
    ckv = _rms(ckv_ref[...], kvn_ref[...])
    ckv_o[...] = ckv
    kpe = _rope(kpe_ref[...], cos, ssin)
    kpe_o[...] = kpe
    kcat_ref[:, 0:A_KV_LORA] = ckv.astype(BF16)
    kcat_ref[:, A_KV_LORA:A_QK] = kpe.astype(BF16)

    q = _dot(_rms(cq_ref[...], qn_ref[...]).astype(BF16), wuq_ref[...])
    for h in range(A_HEADS):
        qh = q[:, h * A_NOPE:(h + 1) * A_NOPE].astype(BF16)
        qcat_ref[h, :, 0:A_KV_LORA] = (_dot(qh, wk_ref[h]) * A_SCALE).astype(BF16)
    pe0 = A_HEADS * A_NOPE
    for pr in range(A_HEADS // 2):
        qp = _rope(q[:, pe0 + pr * LANES:pe0 + (pr + 1) * LANES], cos, ssin) * A_SCALE
        qcat_ref[2 * pr, :, A_KV_LORA:A_QK] = qp.astype(BF16)
        qcat_ref[2 * pr + 1, :, A_KV_LORA:A_QK] = pltpu.roll(qp, A_ROPE, 1).astype(BF16)


def _qkv_call(proj, cos, ssin, qn, kvn, wuq, wk, tm):
    t = proj.shape[0]
    cq_blk = D_MODEL // A_Q_LORA
    return pl.pallas_call(
        _qkv_kernel,
        grid=(t // tm,),
        in_specs=[
            pl.BlockSpec((tm, A_Q_LORA), lambda i: (i, cq_blk)),
            pl.BlockSpec((tm, A_KV_LORA), lambda i: (i, cq_blk + 1)),
            pl.BlockSpec((tm, LANES), lambda i: (i, (D_MODEL + A_Q_LORA + A_KV_LORA) // LANES)),
            pl.BlockSpec((tm, LANES), lambda i: (i, 0)),
            pl.BlockSpec((tm, LANES), lambda i: (i, 0)),
            pl.BlockSpec((1, A_Q_LORA), lambda i: (0, 0)),
            pl.BlockSpec((1, A_KV_LORA), lambda i: (0, 0)),
            pl.BlockSpec(wuq.shape, lambda i: (0, 0)),
            pl.BlockSpec(wk.shape, lambda i: (0, 0, 0)),
        ],
        out_specs=[
            pl.BlockSpec((A_HEADS, tm, A_QK), lambda i: (0, i, 0)),
            pl.BlockSpec((tm, A_QK), lambda i: (i, 0)),
            pl.BlockSpec((tm, A_KV_LORA), lambda i: (i, 0)),
            pl.BlockSpec((tm, LANES), lambda i: (i, 0)),
        ],
        out_shape=[
            jax.ShapeDtypeStruct((A_HEADS, t, A_QK), BF16),
            jax.ShapeDtypeStruct((t, A_QK), BF16),
            jax.ShapeDtypeStruct((t, A_KV_LORA), F32),
            jax.ShapeDtypeStruct((t, LANES), F32),
        ],
        compiler_params=_cparams(("parallel",), 48),
        name="mla_qkv",
    )(proj, proj, proj, cos, ssin, qn.reshape(1, -1), kvn.reshape(1, -1), wuq, wk)


def _attn_prompt_kernel(q_ref, k_ref, gate_ref, wv_ref, o_ref, *scratch, tq, tk, nk, hg):
    i, j = pl.program_id(1), pl.program_id(2)
    groups = A_HEADS // hg
    m_refs, l_refs, acc_refs = scratch[:groups], scratch[groups:2 * groups], scratch[2 * groups:]

    @pl.when(j == 0)
    def _():
        for g in range(groups):
            m_refs[g][...] = jnp.full(m_refs[g].shape, NEG, F32)
            l_refs[g][...] = jnp.zeros(l_refs[g].shape, F32)
            acc_refs[g][...] = jnp.zeros(acc_refs[g].shape, F32)

    def step(masked):
        k = k_ref[...]
        v = k[:, 0:A_KV_LORA]
        def scores(g):
            return _dot_nt(q_ref[g * hg:(g + 1) * hg].reshape(hg * tq, A_QK), k)

        s_next = scores(0)
        for g in range(groups):
            s = s_next
            if g + 1 < groups:
                s_next = scores(g + 1)
            if masked:
                qpos = i * tq + lax.broadcasted_iota(jnp.int32, (hg, tq, tk), 1).reshape(hg * tq, tk)
                kpos = j * tk + lax.broadcasted_iota(jnp.int32, (hg * tq, tk), 1)
                s = jnp.where(kpos <= qpos, s, NEG)
            m_old = m_refs[g][...]
            m_new = jnp.maximum(m_old, jnp.max(s, axis=-1, keepdims=True))
            p = jnp.exp(s - m_new)
            alpha = jnp.exp(m_old - m_new)
            l_refs[g][...] = alpha * l_refs[g][...] + jnp.sum(p, axis=-1, keepdims=True)
            acc_refs[g][...] = alpha * acc_refs[g][...] + _dot(p.astype(BF16), v)
            m_refs[g][...] = m_new

    first_q, last_q = i * tq, i * tq + tq - 1
    pl.when(j * tk + tk - 1 <= first_q)(lambda: step(False))
    pl.when((j * tk <= last_q) & (j * tk + tk - 1 > first_q))(lambda: step(True))

    @pl.when(j == nk - 1)
    def _():
        for h in range(A_HEADS):
            g, sl = h // hg, slice((h % hg) * tq, (h % hg + 1) * tq)
            o_lat = (acc_refs[g][sl, :] / l_refs[g][sl, :]).astype(BF16)
            cols = slice(h * A_VDIM, (h + 1) * A_VDIM)
            o_ref[:, cols] = (_dot(o_lat, wv_ref[h]) * _silu(gate_ref[:, cols])).astype(BF16)


def _attn_prompt_call(qcat, kcat, proj, wv, bsz, seq, tq, tk, hg=A_HEADS):
    nq, nk = seq // tq, seq // tk
    kern = functools.partial(_attn_prompt_kernel, tq=tq, tk=tk, nk=nk, hg=hg)
    groups, rows = A_HEADS // hg, hg * tq
    scratch = ([pltpu.VMEM((rows, 1), F32)] * (2 * groups)) + ([pltpu.VMEM((rows, A_KV_LORA), F32)] * groups)
    return pl.pallas_call(
        kern,
        grid=(bsz, nq, nk),
        in_specs=[
            pl.BlockSpec((A_HEADS, tq, A_QK), lambda b, i, j: (0, b * nq + i, 0)),
            pl.BlockSpec((tk, A_QK), lambda b, i, j: (b * nk + jnp.minimum(j, (i * tq + tq - 1) // tk), 0)),
            pl.BlockSpec((tq, A_WIDTH), lambda b, i, j: (b * nq + i, 0)),
            pl.BlockSpec(wv.shape, lambda b, i, j: (0, 0, 0)),
        ],
        out_specs=pl.BlockSpec((tq, A_WIDTH), lambda b, i, j: (b * nq + i, 0)),
        out_shape=jax.ShapeDtypeStruct((bsz * seq, A_WIDTH), BF16),
        scratch_shapes=scratch,
        compiler_params=_cparams(("parallel", "parallel", "arbitrary"), 56),
        name="mla_attn_prompt",
    )(qcat, kcat, proj, wv)


def _attn_sample_kernel(pt_ref, q_ref, knew_ref, *refs, pages, nsteps, lq, chains):
    ckv_refs, kpe_refs = refs[:pages], refs[pages:2 * pages]
    o_ref, m_ref, l_ref, acc_ref, kbuf, pbuf = refs[2 * pages:]
    c = pl.program_id(1)
    rows = q_ref.shape[1]

    @pl.when(c == 0)
    def _():
        m_ref[...] = jnp.full(m_ref.shape, NEG, F32)
        l_ref[...] = jnp.zeros(l_ref.shape, F32)
        acc_ref[...] = jnp.zeros(acc_ref.shape, F32)

    q = q_ref[0]
    ql, qp = q[:, 0:A_KV_LORA], q[:, A_KV_LORA:A_KV_LORA + A_ROPE]

    def as_col(row):
        return jnp.broadcast_to(row, (rows, rows)).T[:, 0:1]

    def local(st, vals):
        m = jnp.max(st, axis=0, keepdims=True)
        pt = jnp.exp(st - m)
        return m, jnp.sum(pt, axis=0, keepdims=True), _dot_tn(pt.astype(BF16), vals)

    def merge(parts):
        m_old = m_ref[...]
        m_new = m_old
        for m, _, _ in parts:
            m_new = jnp.maximum(m_new, m)
        alpha = jnp.exp(m_old - m_new)
        l_new = alpha * l_ref[...]
        acc = as_col(alpha) * acc_ref[...]
        for m, l, o in parts:
            w = jnp.exp(m - m_new)
            l_new = l_new + w * l
            acc = acc + as_col(w) * o
        m_ref[...], l_ref[...], acc_ref[...] = m_new, l_new, acc

    per = pages // chains

    def scores(ch):
        for n in range(ch * per, (ch + 1) * per):
            kbuf[n * PAGE_SIZE:(n + 1) * PAGE_SIZE, :] = ckv_refs[n][0, 0].astype(BF16)
            pbuf[n * PAGE_SIZE:(n + 1) * PAGE_SIZE, :] = kpe_refs[n][0, 0].astype(BF16)
        sl = slice(ch * per * PAGE_SIZE, (ch + 1) * per * PAGE_SIZE)
        keys = kbuf[sl, :]
        return _dot_nt(keys, ql) + _dot_nt(pbuf[sl, :], qp), keys

    parts = []
    nxt = scores(0)
    for ch in range(chains):
        st, keys = nxt
        if ch + 1 < chains:
            nxt = scores(ch + 1)
        parts.append(local(st, keys))
    merge(parts)

    @pl.when(c == nsteps - 1)
    def _():
        kn = jnp.concatenate([knew_ref[0], jnp.zeros((PAGE_SIZE - SUBLANES, A_QK), F32)], axis=0).astype(BF16)
        t_k = lax.broadcasted_iota(jnp.int32, (PAGE_SIZE, rows), 0)
        t_q = lax.broadcasted_iota(jnp.int32, (PAGE_SIZE, rows), 1) % lq
        merge([local(jnp.where(t_k <= t_q, _dot_nt(kn, q), NEG), kn[:, 0:A_KV_LORA])])
        o_ref[0] = acc_ref[...] / as_col(l_ref[...])


def _attn_sample_call(page_table, q, knew, cache_ckv, cache_kpe, layer, pages, chains=2):
    nb, n_pages = page_table.shape
    rows = q.shape[1]
    lq = knew.shape[1]
    nsteps = n_pages // pages
    kern = functools.partial(_attn_sample_kernel, pages=pages, nsteps=nsteps, lq=lq, chains=chains)
    knew = jnp.pad(knew, ((0, 0), (0, SUBLANES - lq), (0, 0)))

    def page_map(n):
        return lambda b, c, pt: (layer, pt[b, c * pages + n], 0, 0)

    in_specs = [
        pl.BlockSpec((1, rows, A_QK), lambda b, c, pt: (b, 0, 0)),
        pl.BlockSpec((1, SUBLANES, A_QK), lambda b, c, pt: (b, 0, 0)),
    ]
    in_specs += [pl.BlockSpec((1, 1, PAGE_SIZE, A_KV_LORA), page_map(n)) for n in range(pages)]
    in_specs += [pl.BlockSpec((1, 1, PAGE_SIZE, A_ROPE), page_map(n)) for n in range(pages)]
    grid_spec = pltpu.PrefetchScalarGridSpec(
        num_scalar_prefetch=1,
        grid=(nb, nsteps),
        in_specs=in_specs,
        out_specs=pl.BlockSpec((1, rows, A_KV_LORA), lambda b, c, pt: (b, 0, 0)),
        scratch_shapes=[pltpu.VMEM((1, rows), F32), pltpu.VMEM((1, rows), F32), pltpu.VMEM((rows, A_KV_LORA), F32),
                        pltpu.VMEM((pages * PAGE_SIZE, A_KV_LORA), BF16), pltpu.VMEM((pages * PAGE_SIZE, A_ROPE), BF16)],
    )
    return pl.pallas_call(
        kern,
        grid_spec=grid_spec,
        out_shape=jax.ShapeDtypeStruct((nb, rows, A_KV_LORA), F32),
        compiler_params=_cparams(("parallel", "arbitrary"), 48),
        name="mla_attn_sample",
    )(page_table, q, knew, *([cache_ckv] * pages), *([cache_kpe] * pages))


def _oproj_kernel(o_ref, wv_ref, gate_ref, out_ref):
    out_ref[...] = (_dot(o_ref[0].astype(BF16), wv_ref[0]) * _silu(gate_ref[...])).astype(BF16)


def _oproj_call(o_lat, wv, proj):
    t = o_lat.shape[1]
    return pl.pallas_call(
        _oproj_kernel,
        grid=(A_HEADS,),
        in_specs=[
            pl.BlockSpec((1, t, A_KV_LORA), lambda h: (h, 0, 0)),
            pl.BlockSpec((1, A_KV_LORA, A_VDIM), lambda h: (h, 0, 0)),
            pl.BlockSpec((t, A_VDIM), lambda h: (0, h)),
        ],
        out_specs=pl.BlockSpec((t, A_VDIM), lambda h: (0, h)),
        out_shape=jax.ShapeDtypeStruct((t, A_WIDTH), BF16),
        compiler_params=_cparams(("parallel",), 32),
        name="mla_oproj_sample",
    )(o_lat, wv, proj)


def _group_norm(y, nw):
    outs = []
    for g in range(B_GROUPS):
        sl = slice(g * B_GW, (g + 1) * B_GW)
        outs.append(_rms(y[:, sl], nw[:, sl]))
    return jnp.concatenate(outs, axis=1)


def _ssd_prompt_kernel(proj_ref, cw_ref, cb_ref, dtb_ref, alog_ref, dexp_ref, nw_ref, e_ref, tril_ref,
                       y_ref, tail_ref, hlast_ref, xp_ref, h_ref, yacc_ref, *, nchunks):
    c = pl.program_id(1)
    q = CHUNK

    @pl.when(c == 0)
    def _():
        xp_ref[0:SUBLANES, :] = jnp.zeros((SUBLANES, B_CONV_DIM), F32)
        h_ref[...] = jnp.zeros(h_ref.shape, F32)

    raw = proj_ref[:, B_WIDTH:B_WIDTH + B_CONV_DIM]
    xp_ref[SUBLANES:SUBLANES + q, :] = raw
    conv = cb_ref[...] + cw_ref[3:4, :] * raw
    for back in range(1, B_CONV):
        conv = conv + cw_ref[3 - back:4 - back, :] * xp_ref[SUBLANES - back:SUBLANES - back + q, :]
    xp_ref[0:SUBLANES, :] = raw[q - SUBLANES:q, :]
    tail_ref[0] = raw[q - SUBLANES:q, :]
    xc = _silu(conv)
    x = xc[:, 0:B_WIDTH]
    bm = xc[:, B_WIDTH:B_WIDTH + B_GN].astype(BF16)
    cm = xc[:, B_WIDTH + B_GN:B_CONV_DIM].astype(BF16)

    dt = _softplus(proj_ref[:, B_WIDTH + B_CONV_DIM:B_IN_PAD] + dtb_ref[...])
    dta = dt * (-jnp.exp(alog_ref[...]))
    acum = _split_dot(dta, tril_ref[...], 3, left=True)
    acum_t = acum.T
    alast = acum[q - 1:q, :]
    e = e_ref[...]
    dtx = _split_dot(dt, e, 2)
    eax = _split_dot(jnp.exp(acum), e, 2)
    wx = _split_dot(jnp.exp(alast - acum) * dt, e, 2)
    lane = lax.broadcasted_iota(jnp.int32, (q, B_WIDTH), 1)
    even = (lane % LANES) < B_HEADDIM
    xdt = x * dtx
    xdt_even = jnp.where(even, xdt, 0.0).astype(BF16)
    xdt_odd = jnp.where(even, 0.0, xdt).astype(BF16)
    xw = (x * wx).astype(BF16)
    causal = lax.broadcasted_iota(jnp.int32, (q, q), 0) >= lax.broadcasted_iota(jnp.int32, (q, q), 1)
    dec_col = jnp.exp(acum_t[:, q - 1:q])

    for g in range(B_GROUPS):
        cg = cm[:, g * B_STATE:(g + 1) * B_STATE]
        bg = bm[:, g * B_STATE:(g + 1) * B_STATE]
        cb = _dot_nt(cg, bg)
        hg = h_ref[g * B_HPG:(g + 1) * B_HPG].reshape(B_GW, B_STATE)
        ystate = _dot_nt(cg, hg.astype(BF16))
        for pr in range(B_HPG // 2):
            ls = []
            for hh in (2 * pr, 2 * pr + 1):
                h = g * B_HPG + hh
                diff = acum[:, h:h + 1] - acum_t[h:h + 1, :]
                ls.append((cb * jnp.exp(jnp.where(causal, diff, NEG))).astype(BF16))
            col = g * B_GW + pr * LANES
            lhs = jnp.concatenate(ls, axis=1)
            rhs = jnp.concatenate([xdt_even[:, col:col + LANES], xdt_odd[:, col:col + LANES]], axis=0)
            yacc_ref[:, col:col + LANES] = _dot(lhs, rhs) + ystate[:, pr * LANES:(pr + 1) * LANES] * eax[:, col:col + LANES]
        snew = _dot_tn(xw[:, g * B_GW:(g + 1) * B_GW], bg)
        for hh in range(B_HPG):
            h = g * B_HPG + hh
            dec = jnp.broadcast_to(dec_col[h:h + 1, :], (B_HEADDIM, B_STATE))
            h_ref[h] = h_ref[h] * dec + snew[hh * B_HEADDIM:(hh + 1) * B_HEADDIM, :]

    y = (yacc_ref[...] + dexp_ref[...] * x) * _silu(proj_ref[:, 0:B_WIDTH])
    y_ref[...] = _group_norm(y, nw_ref[...]).astype(BF16)

    @pl.when(c == nchunks - 1)
    def _():
        hlast_ref[0] = h_ref[...]


def _ssd_prompt_call(proj, cw, cb, dtb, alog, dexp, nw, e, tril, bsz, seq):
    nchunks = seq // CHUNK
    kern = functools.partial(_ssd_prompt_kernel, nchunks=nchunks)
    const = lambda b, c: (0, 0)
    return pl.pallas_call(
        kern,
        grid=(bsz, nchunks),
        in_specs=[
            pl.BlockSpec((CHUNK, B_IN_PAD), lambda b, c: (b * nchunks + c, 0)),
            pl.BlockSpec((B_CONV, B_CONV_DIM), const),
            pl.BlockSpec((1, B_CONV_DIM), const),
            pl.BlockSpec((1, LANES), const),
            pl.BlockSpec((1, LANES), const),
            pl.BlockSpec((1, B_WIDTH), const),
            pl.BlockSpec((1, B_WIDTH), const),
            pl.BlockSpec((LANES, B_WIDTH), const),
            pl.BlockSpec((CHUNK, CHUNK), const),
        ],
        out_specs=[
            pl.BlockSpec((CHUNK, B_WIDTH), lambda b, c: (b * nchunks + c, 0)),
            pl.BlockSpec((1, SUBLANES, B_CONV_DIM), lambda b, c: (b, 0, 0)),
            pl.BlockSpec((1, B_HEADS, B_HEADDIM, B_STATE), lambda b, c: (b, 0, 0, 0)),
        ],
        out_shape=[
            jax.ShapeDtypeStruct((bsz * seq, B_WIDTH), BF16),
            jax.ShapeDtypeStruct((bsz, SUBLANES, B_CONV_DIM), F32),
            jax.ShapeDtypeStruct((bsz, B_HEADS, B_HEADDIM, B_STATE), F32),
        ],
        scratch_shapes=[
            pltpu.VMEM((SUBLANES + CHUNK, B_CONV_DIM), F32),
            pltpu.VMEM((B_HEADS, B_HEADDIM, B_STATE), F32),
            pltpu.VMEM((CHUNK, B_WIDTH), F32),
        ],
        compiler_params=_cparams(("parallel", "arbitrary"), 56),
        name="ssd_prompt",
    )(proj, cw, cb, dtb, alog, dexp, nw, e, tril)


def _ssd_sample_conv_kernel(x_ref, prev_ref, cw_ref, cb_ref, o_ref, *, lq):
    rows = [prev_ref[n] for n in range(B_CONV - 1)] + [x_ref[n] for n in range(lq)]
    for t in range(lq):
        conv = cb_ref[...]
        for j in range(B_CONV):
            conv = conv + cw_ref[j:j + 1, :] * rows[t + j]
        o_ref[t] = _silu(conv)


def _ssd_sample_conv_call(proj_t, prev_t, cw, cb):
    lq, nb = proj_t.shape[:2]
    tc = 2048
    off = B_WIDTH // tc
    kern = functools.partial(_ssd_sample_conv_kernel, lq=lq)
    return pl.pallas_call(
        kern,
        grid=(B_CONV_DIM // tc,),
        in_specs=[
            pl.BlockSpec((lq, nb, tc), lambda j: (0, 0, off + j)),
            pl.BlockSpec((B_CONV - 1, nb, tc), lambda j: (0, 0, j)),
            pl.BlockSpec((B_CONV, tc), lambda j: (0, j)),
            pl.BlockSpec((1, tc), lambda j: (0, j)),
        ],
        out_specs=pl.BlockSpec((lq, nb, tc), lambda j: (0, 0, j)),
        out_shape=jax.ShapeDtypeStruct((lq, nb, B_CONV_DIM), F32),
        compiler_params=_cparams(("parallel",), 40),
        name="ssd_sample_conv",
    )(proj_t, prev_t, cw, cb)


def _ssd_sample_intra_kernel(xc_ref, dt_ref, dtb_ref, alog_ref, dexp_ref, r_ref, e_ref,
                             yin_ref, eax_ref, xw_ref, dec_ref, *, lq):
    qi = pl.program_id(0)
    a = -jnp.exp(alog_ref[...])
    dts = [_softplus(dt_ref[k] + dtb_ref[...]) for k in range(lq)]
    acums = []
    run = None
    for k in range(lq):
        run = dts[k] * a if run is None else run + dts[k] * a
        acums.append(run)
    acum_q = acums[0]
    for k in range(1, lq):
        acum_q = jnp.where(qi >= k, acums[k], acum_q)
    e, r = e_ref[...], r_ref[...]
    x_q = xc_ref[qi, :, 0:B_WIDTH]
    c_q = xc_ref[qi, :, B_WIDTH + B_GN:B_CONV_DIM]
    y = dexp_ref[...] * x_q
    for k in range(lq):
        b_k = xc_ref[k, :, B_WIDTH:B_WIDTH + B_GN]
        cb = _split_dot(c_q * b_k, r, 2)
        m = cb * jnp.exp(jnp.where(qi >= k, acum_q - acums[k], NEG)) * dts[k]
        y = y + _split_dot(m, e, 2) * xc_ref[k, :, 0:B_WIDTH]
    yin_ref[0] = y
    eax_ref[0] = _split_dot(jnp.exp(acum_q), e, 2)
    dt_q = dts[0]
    for k in range(1, lq):
        dt_q = jnp.where(qi == k, dts[k], dt_q)
    xw_ref[0] = x_q * _split_dot(jnp.exp(acums[lq - 1] - acum_q) * dt_q, e, 2)
    dec_ref[...] = jnp.exp(acums[lq - 1])


def _ssd_sample_intra_call(xc, proj_t, dtb, alog, dexp, r, e):
    lq, nb = xc.shape[:2]
    kern = functools.partial(_ssd_sample_intra_kernel, lq=lq)
    tok = lambda q: (q, 0, 0)
    const = lambda q: (0, 0)
    big = jax.ShapeDtypeStruct((lq, nb, B_WIDTH), F32)
    return pl.pallas_call(
        kern,
        grid=(lq,),
        in_specs=[
            pl.BlockSpec((lq, nb, B_CONV_DIM), lambda q: (0, 0, 0)),
            pl.BlockSpec((lq, nb, LANES), lambda q: (0, 0, (B_WIDTH + B_CONV_DIM) // LANES)),
            pl.BlockSpec((1, LANES), const),
            pl.BlockSpec((1, LANES), const),
            pl.BlockSpec((1, B_WIDTH), const),
            pl.BlockSpec((B_GN, LANES), const),
            pl.BlockSpec((LANES, B_WIDTH), const),
        ],
        out_specs=[
            pl.BlockSpec((1, nb, B_WIDTH), tok),
            pl.BlockSpec((1, nb, B_WIDTH), tok),
            pl.BlockSpec((1, nb, B_WIDTH), tok),
            pl.BlockSpec((nb, LANES), const),
        ],
        out_shape=[big, big, big, jax.ShapeDtypeStruct((nb, LANES), F32)],
        compiler_params=_cparams(("arbitrary",), 56),
        name="ssd_sample_intra",
    )(xc, proj_t, dtb, alog, dexp, r, e)


def _ssd_sample_state_kernel(dec_ref, h0_ref, c_ref, b_ref, xw_ref, yin_ref, eax_ref, z_ref, nw_ref,
                             y_ref, hl_ref, c8, b8, xw8, *, lq):
    s = pl.program_id(0)
    r = pl.ds(s % SUBLANES, 1)
    for buf in (c8, b8, xw8):
        buf[lq:SUBLANES, :] = jnp.zeros((SUBLANES - lq, buf.shape[1]), F32)
    for t in range(lq):
        c8[t:t + 1, :] = c_ref[t, r, :]
        b8[t:t + 1, :] = b_ref[t, r, :]
        xw8[t:t + 1, :] = xw_ref[t, r, :]
    ys = []
    for g in range(B_GROUPS):
        hg = h0_ref[0, g * B_HPG:(g + 1) * B_HPG].reshape(B_GW, B_STATE).astype(BF16)
        cg = c8[:, g * B_STATE:(g + 1) * B_STATE].astype(BF16)
        bg = b8[:, g * B_STATE:(g + 1) * B_STATE].astype(BF16)
        ys.append(_dot_nt(cg, hg))
        snew = _dot_tn(xw8[:, g * B_GW:(g + 1) * B_GW].astype(BF16), bg)
        for hh in range(B_HPG):
            h = g * B_HPG + hh
            hl_ref[0, h] = h0_ref[0, h] * dec_ref[s, h] + snew[hh * B_HEADDIM:(hh + 1) * B_HEADDIM, :]
    ystate = jnp.concatenate(ys, axis=1)
    for t in range(lq):
        y = (yin_ref[t, r, :] + ystate[t:t + 1, :] * eax_ref[t, r, :]) * _silu(z_ref[t, r, :])
        y_ref[0, t:t + 1, :] = _group_norm(y, nw_ref[...])


def _ssd_sample_state_call(dec, h0, xc, xw, yin, eax, proj_t, nw):
    lq, nb = xc.shape[:2]
    kern = functools.partial(_ssd_sample_state_kernel, lq=lq)
    row = lambda blk: (lambda s, d: (0, s // SUBLANES, blk))
    grid_spec = pltpu.PrefetchScalarGridSpec(
        num_scalar_prefetch=1,
        grid=(nb,),
        in_specs=[
            pl.BlockSpec((1, B_HEADS, B_HEADDIM, B_STATE), lambda s, d: (s, 0, 0, 0)),
            pl.BlockSpec((lq, SUBLANES, B_GN), row((B_WIDTH + B_GN) // B_GN)),
            pl.BlockSpec((lq, SUBLANES, B_GN), row(B_WIDTH // B_GN)),
            pl.BlockSpec((lq, SUBLANES, B_WIDTH), row(0)),
            pl.BlockSpec((lq, SUBLANES, B_WIDTH), row(0)),
            pl.BlockSpec((lq, SUBLANES, B_WIDTH), row(0)),
            pl.BlockSpec((lq, SUBLANES, B_WIDTH), row(0)),
            pl.BlockSpec((1, B_WIDTH), lambda s, d: (0, 0)),
        ],
        out_specs=[
            pl.BlockSpec((1, lq, B_WIDTH), lambda s, d: (s, 0, 0)),
            pl.BlockSpec((1, B_HEADS, B_HEADDIM, B_STATE), lambda s, d: (s, 0, 0, 0)),
        ],
        scratch_shapes=[pltpu.VMEM((SUBLANES, B_GN), F32), pltpu.VMEM((SUBLANES, B_GN), F32),
                        pltpu.VMEM((SUBLANES, B_WIDTH), F32)],
    )
    return pl.pallas_call(
        kern,
        grid_spec=grid_spec,
        out_shape=[jax.ShapeDtypeStruct((nb, lq, B_WIDTH), F32),
                   jax.ShapeDtypeStruct((nb, B_HEADS, B_HEADDIM, B_STATE), F32)],
        compiler_params=_cparams(("arbitrary",), 40),
        name="ssd_sample_state",
    )(dec, h0, xc, xc, xw, yin, eax, proj_t, nw)


def _gmlp_kernel(p_ref, wm_ref, bias_ref, lng_ref, lnb_ref, o_ref, *v_refs):
    q = CHUNK
    u = _gelu(p_ref[:, 0:C_WIDTH])
    vp = _gelu(p_ref[:, C_WIDTH:2 * C_WIDTH])
    mu = jnp.mean(vp, axis=-1, keepdims=True)
    var = jnp.mean(jnp.square(vp - mu), axis=-1, keepdims=True)
    v = (vp - mu) * lax.rsqrt(var + EPS) * lng_ref[...] + lnb_ref[...]
    if v_refs:
        v_refs[0][...] = v
    vb = v.astype(BF16)
    causal = lax.broadcasted_iota(jnp.int32, (q, q), 0) >= lax.broadcasted_iota(jnp.int32, (q, q), 1)
    for g in range(C_GROUPS):
        sl = slice(g * C_GDIM, (g + 1) * C_GDIM)
        w = jnp.where(causal, wm_ref[g], 0.0).astype(BF16)
        mix = _dot(w, vb[:, sl]) + bias_ref[:, sl]
        gate = p_ref[:, 2 * C_WIDTH + g * C_GDIM:2 * C_WIDTH + (g + 1) * C_GDIM]
        o_ref[:, sl] = (u[:, sl] * mix * _silu(gate)).astype(BF16)


def _gmlp_call(proj, wm, bias, lng, lnb, want_v):
    t = proj.shape[0]
    row = lambda i: (i, 0)
    const = lambda i: (0, 0)
    out_specs = [pl.BlockSpec((CHUNK, C_WIDTH), row)]
    out_shape = [jax.ShapeDtypeStruct((t, C_WIDTH), BF16)]
    if want_v:
        out_specs.append(pl.BlockSpec((CHUNK, C_WIDTH), row))
        out_shape.append(jax.ShapeDtypeStruct((t, C_WIDTH), F32))
    return pl.pallas_call(
        _gmlp_kernel,
        grid=(t // CHUNK,),
        in_specs=[
            pl.BlockSpec((CHUNK, 3 * C_WIDTH), row),
            pl.BlockSpec((C_GROUPS, CHUNK, CHUNK), lambda i: (0, 0, 0)),
            pl.BlockSpec((CHUNK, C_WIDTH), const),
            pl.BlockSpec((1, C_WIDTH), const),
            pl.BlockSpec((1, C_WIDTH), const),
        ],
        out_specs=out_specs,
        out_shape=out_shape,
        compiler_params=_cparams(("parallel",), 48),
        name="gmlp_mix",
    )(proj, wm, bias, lng.reshape(1, C_WIDTH), lnb.reshape(1, C_WIDTH))


def _pad_cols(w, n):
    return jnp.pad(w, ((0, 0), (0, n - w.shape[1])))


def _rope_tables(pos):
    half = A_ROPE // 2
    inv = ROPE_THETA ** (-jnp.arange(half, dtype=F32) / half)
    ang = pos.astype(F32)[:, None] * inv[None, :]
    cos, sin = jnp.cos(ang), jnp.sin(ang)
    return jnp.tile(cos, (1, LANES // half)), jnp.tile(jnp.concatenate([-sin, sin], axis=1), (1, LANES // A_ROPE))


def _head_expand():
    rows = lax.broadcasted_iota(jnp.int32, (LANES, B_WIDTH), 0)
    cols = lax.broadcasted_iota(jnp.int32, (LANES, B_WIDTH), 1)
    return (cols // B_HEADDIM == rows).astype(BF16)


def _group_reduce():
    rows = lax.broadcasted_iota(jnp.int32, (B_GN, LANES), 0)
    cols = lax.broadcasted_iota(jnp.int32, (B_GN, LANES), 1)
    return ((rows // B_STATE == cols // B_HPG) & (cols < B_HEADS)).astype(BF16)


def kernel(x_prompt, x_sample, c_prompt, c_sample, cache_ckv, cache_kpe, page_table, state_ssm, state_conv, ada_w, ada_b, norm_w, norm_f, a_w_in, a_q_norm, a_w_uq, a_kv_norm, a_w_ukv, a_w_out, b_w_in, b_conv_w, b_conv_b, b_dt_bias, b_a_log, b_d, b_norm, b_w_out, c_w_in, c_ln_g, c_ln_b, c_ws, c_bs, c_w_out):
    bsz, seq, _ = x_prompt.shape
    nb, lq, _ = x_sample.shape
    tp, ts = bsz * seq, nb * lq
    past_len = page_table.shape[1] * PAGE_SIZE

    mods = _ada_call(jnp.concatenate([c_prompt, c_sample], axis=0), ada_w, ada_b)
    e_mat, r_mat = _head_expand(), _group_reduce()
    tril = (lax.broadcasted_iota(jnp.int32, (CHUNK, CHUNK), 0)
            >= lax.broadcasted_iota(jnp.int32, (CHUNK, CHUNK), 1)).astype(BF16)
    cos_p, sin_p = _rope_tables(jnp.tile(jnp.arange(seq, dtype=jnp.int32), bsz))
    cos_s, sin_s = _rope_tables(jnp.tile(past_len + jnp.arange(lq, dtype=jnp.int32), nb))

    xp = x_prompt.reshape(tp, D_MODEL)
    xs = x_sample.reshape(ts, D_MODEL)
    tm_p, tm_s = 1024, ts
    outs = dict(ckv_p=[], kpe_p=[], ckv_s=[], kpe_s=[], ssm_p=[], conv_p=[], ssm_s=[], conv_s=[], v_s=[])

    for l in range(DEPTH):
        kind, j = l % 3, l // 3
        mods_p = mods[l, :bsz].reshape(bsz, 1, 3 * D_MODEL)
        mods_s = jnp.repeat(mods[l, bsz:], lq, axis=0).reshape(1, ts, 3 * D_MODEL)
        if kind == 0:
            o1, o2, o3 = A_Q_LORA, A_Q_LORA + A_KV_LORA, A_Q_LORA + A_KV_LORA + A_ROPE
            w = a_w_in[j]
            w_in = _pad_cols(jnp.concatenate([w[:, o3:], w[:, :o3]], axis=1), A_IN_PAD).astype(BF16)
            wuq = a_w_uq[j].reshape(A_Q_LORA, A_HEADS, A_NOPE + A_ROPE)
            wuq = jnp.concatenate([wuq[:, :, :A_NOPE].reshape(A_Q_LORA, -1),
                                   wuq[:, :, A_NOPE:].reshape(A_Q_LORA, -1)], axis=1).astype(BF16)
            wk = a_w_ukv[j][:, :, :A_NOPE].transpose(1, 2, 0).astype(BF16)
            wv = a_w_ukv[j][:, :, A_NOPE:].transpose(1, 0, 2).astype(BF16)
            w_out = a_w_out[j].astype(BF16)

            proj_p = _in_call(xp, norm_w[l], mods_p, w_in, tm_p, 640, seq)
            qcat, kcat, ckv, kpe = _qkv_call(proj_p, cos_p, sin_p, a_q_norm[j], a_kv_norm[j], wuq, wk, 256)
            a_p = _attn_prompt_call(qcat, kcat, proj_p, wv, bsz, seq, 128, 512, 2)
            xp = _out_call(a_p, w_out, xp, mods_p, tm_p, 512, seq)
            outs["ckv_p"].append(ckv.reshape(bsz, seq, A_KV_LORA))
            outs["kpe_p"].append(kpe[:, :A_ROPE].reshape(bsz, seq, A_ROPE))

            proj_s = _in_call(xs, norm_w[l], mods_s, w_in, tm_s, 640, ts)
            qcat, kcat, ckv, kpe = _qkv_call(proj_s, cos_s, sin_s, a_q_norm[j], a_kv_norm[j], wuq, wk, 256)
            rows_s = A_HEADS * lq
            q_s = qcat.reshape(A_HEADS, nb, lq, A_QK).transpose(1, 0, 2, 3).reshape(nb, rows_s, A_QK)
            q_s = jnp.pad(q_s, ((0, 0), (0, -rows_s % LANES), (0, 0)))
            knew = jnp.concatenate([ckv, kpe], axis=1).reshape(nb, lq, A_QK)
            o_lat = _attn_sample_call(page_table, q_s, knew, cache_ckv, cache_kpe, j, 16)[:, :rows_s]
            o_lat = o_lat.reshape(nb, A_HEADS, lq, A_KV_LORA).transpose(1, 0, 2, 3).reshape(A_HEADS, ts, A_KV_LORA)
            a_s = _oproj_call(o_lat, wv, proj_s)
            xs = _out_call(a_s, w_out, xs, mods_s, tm_s, 512, ts)
            outs["ckv_s"].append(ckv.reshape(nb, lq, A_KV_LORA))
            outs["kpe_s"].append(kpe[:, :A_ROPE].reshape(nb, lq, A_ROPE))
        elif kind == 1:
            w_in = _pad_cols(b_w_in[j], B_IN_PAD).astype(BF16)
            w_out = b_w_out[j].astype(BF16)
            cw, cb = b_conv_w[j], b_conv_b[j].reshape(1, B_CONV_DIM)
            dtb = jnp.pad(b_dt_bias[j], (0, LANES - B_HEADS)).reshape(1, LANES)
            alog = jnp.pad(b_a_log[j], (0, LANES - B_HEADS)).reshape(1, LANES)
            dexp = jnp.repeat(b_d[j], B_HEADDIM).reshape(1, B_WIDTH)
            nw = b_norm[j].reshape(1, B_WIDTH)

            proj_p = _in_call(xp, norm_w[l], mods_p, w_in, tm_p, 1152, seq)
            y_p, tail, h_last = _ssd_prompt_call(proj_p, cw, cb, dtb, alog, dexp, nw, e_mat, tril, bsz, seq)
            xp = _out_call(y_p, w_out, xp, mods_p, tm_p, 512, seq)
            outs["ssm_p"].append(h_last)
            outs["conv_p"].append(tail[:, SUBLANES - (B_CONV - 1):])

            proj_s = _in_call(xs, norm_w[l], mods_s, w_in, tm_s, 1152, ts)
            proj_t = proj_s.reshape(nb, lq, B_IN_PAD).transpose(1, 0, 2)
            prev_t = state_conv[j].transpose(1, 0, 2)
            xc = _ssd_sample_conv_call(proj_t, prev_t, cw, cb)
            yin, eax, xw, dec = _ssd_sample_intra_call(xc, proj_t, dtb, alog, dexp, r_mat, e_mat)
            y_s, h_last = _ssd_sample_state_call(dec, state_ssm[j], xc, xw, yin, eax, proj_t, nw)
            xs = _out_call(y_s.reshape(ts, B_WIDTH), w_out, xs, mods_s, tm_s, 512, ts)
            raw = jnp.concatenate([state_conv[j], proj_s[:, B_WIDTH:B_WIDTH + B_CONV_DIM].reshape(nb, lq, B_CONV_DIM)], axis=1)
            outs["ssm_s"].append(h_last)
            outs["conv_s"].append(raw[:, -(B_CONV - 1):])
        else:
            w_in = c_w_in[j].astype(BF16)
            w_out = c_w_out[j].astype(BF16)
            bias_p = jnp.repeat(c_bs[j].T, C_GDIM, axis=1)
            proj_p = _in_call(xp, norm_w[l], mods_p, w_in, tm_p, 1024, seq)
            (m_p,) = _gmlp_call(proj_p, c_ws[j], bias_p, c_ln_g[j], c_ln_b[j], False)
            xp = _out_call(m_p, w_out, xp, mods_p, tm_p, 512, seq)

            reps = CHUNK // lq
            wm_s = jnp.einsum("ab,gts->gatbs", jnp.eye(reps, dtype=F32), c_ws[j][:, :lq, :lq]).reshape(C_GROUPS, CHUNK, CHUNK)
            bias_s = jnp.tile(jnp.repeat(c_bs[j][:, :lq].T, C_GDIM, axis=1), (reps, 1))
            proj_s = _in_call(xs, norm_w[l], mods_s, w_in, tm_s, 1024, ts)
            m_s, v_s = _gmlp_call(proj_s, wm_s, bias_s, c_ln_g[j], c_ln_b[j], True)
            xs = _out_call(m_s, w_out, xs, mods_s, tm_s, 512, ts)
            outs["v_s"].append(v_s.reshape(nb, lq, C_WIDTH))

    y_p = _norm_call(xp, norm_f, 1024).reshape(bsz, seq, D_MODEL)
    y_s = _norm_call(xs, norm_f, ts).reshape(nb, lq, D_MODEL)
    return (y_p, y_s,
            jnp.stack(outs["ckv_p"]), jnp.stack(outs["kpe_p"]), jnp.stack(outs["ckv_s"]), jnp.stack(outs["kpe_s"]),
            jnp.stack(outs["ssm_p"]), jnp.stack(outs["conv_p"]), jnp.stack(outs["ssm_s"]), jnp.stack(outs["conv_s"]),
            jnp.stack(outs["v_s"]))
```

```python
import functools
import math

import jax
import jax.numpy as jnp
from jax import lax
from jax.experimental import pallas as pl
from jax.experimental.pallas import tpu as pltpu

F32 = jnp.float32
BF16 = jnp.bfloat16

D_MODEL = 2048
DEPTH = 4
EPS = 1e-6
A_HEADS = 16
A_NOPE = 128
A_ROPE = 64
A_VDIM = 128
A_Q_LORA = 512
A_KV_LORA = 512
A_WIDTH = A_HEADS * A_VDIM
A_SCALE = (A_NOPE + A_ROPE) ** -0.5
ROPE_THETA = 10000.0
PAGE_SIZE = 128
B_WIDTH = 2 * D_MODEL
B_HEADDIM = 64
B_HEADS = B_WIDTH // B_HEADDIM
B_GROUPS = 8
B_STATE = 128
B_CONV = 4
B_GN = B_GROUPS * B_STATE
B_CONV_DIM = B_WIDTH + 2 * B_GN
B_HPG = B_HEADS // B_GROUPS
B_GW = B_WIDTH // B_GROUPS
CHUNK = 128
C_WIDTH = 2 * D_MODEL
C_GROUPS = 16
C_GDIM = C_WIDTH // C_GROUPS

LANES = 128
SUBLANES = 8
VMEM_BYTES = 64 * 1024 * 1024

A_IN_PAD = 3200
A_QK = A_KV_LORA + LANES
B_IN_PAD = B_WIDTH + B_CONV_DIM + LANES
NEG = -1e30

NT_DIMS = (((1,), (1,)), ((), ()))
TN_DIMS = (((0,), (0,)), ((), ()))


def _cparams(sem, vmem_mb):
    return pltpu.CompilerParams(dimension_semantics=sem, vmem_limit_bytes=vmem_mb * 1024 * 1024)


def _silu(x):
    return x * (1.0 / (1.0 + jnp.exp(-x)))


def _gelu(x):
    return 0.5 * x * (1.0 + jnp.tanh(math.sqrt(2.0 / math.pi) * (x + 0.044715 * (x * x * x))))


def _softplus(x):
    return jnp.maximum(x, 0.0) + jnp.log1p(jnp.exp(-jnp.abs(x)))


def _rms(x, w):
    return x * lax.rsqrt(jnp.mean(x * x, axis=-1, keepdims=True) + EPS) * w


def _dot(a, b):
    return jnp.dot(a, b, preferred_element_type=F32)


def _dot_nt(a, b):
    return lax.dot_general(a, b, NT_DIMS, preferred_element_type=F32)


def _dot_tn(a, b):
    return lax.dot_general(a, b, TN_DIMS, preferred_element_type=F32)


def _split_dot(a, sel, parts, left=False):
    acc = None
    rem = a
    for _ in range(parts):
        piece = rem.astype(BF16)
        rem = rem - piece.astype(F32)
        term = _dot(sel, piece) if left else _dot(piece, sel)
        acc = term if acc is None else acc + term
    return acc


def _ada_kernel(c_ref, w_ref, b_ref, o_ref):
    s = _silu(c_ref[...]).astype(BF16)
    o_ref[0] = _dot(s, w_ref[0].astype(BF16)) + b_ref[0]


def _ada_call(c_all, ada_w, ada_b):
    n_rows = c_all.shape[0]
    tn = 1024
    return pl.pallas_call(
        _ada_kernel,
        grid=(DEPTH, 3 * D_MODEL // tn),
        in_specs=[
            pl.BlockSpec((n_rows, D_MODEL), lambda l, j: (0, 0)),
            pl.BlockSpec((1, D_MODEL, tn), lambda l, j: (l, 0, j)),
            pl.BlockSpec((1, 1, tn), lambda l, j: (l, 0, j)),
        ],
        out_specs=pl.BlockSpec((1, n_rows, tn), lambda l, j: (l, 0, j)),
        out_shape=jax.ShapeDtypeStruct((DEPTH, n_rows, 3 * D_MODEL), F32),
        compiler_params=_cparams(("parallel", "parallel"), 40),
        name="ada_mod",
    )(c_all, ada_w, ada_b.reshape(DEPTH, 1, 3 * D_MODEL))


def _in_kernel(x_ref, nw_ref, sh_ref, sc_ref, w_ref, o_ref, h_ref):
    @pl.when(pl.program_id(1) == 0)
    def _():
        y = _rms(x_ref[...], nw_ref[...])
        h_ref[...] = (y * (1.0 + sc_ref[0]) + sh_ref[0]).astype(BF16)

    o_ref[...] = _dot(h_ref[...], w_ref[...])


def _in_call(x, nw, mods, w, tm, tn, rows_per_mod):
    t, n = x.shape[0], w.shape[1]
    r = mods.shape[1]
    return pl.pallas_call(
        _in_kernel,
        grid=(t // tm, n // tn),
        in_specs=[
            pl.BlockSpec((tm, D_MODEL), lambda i, j: (i, 0)),
            pl.BlockSpec((1, D_MODEL), lambda i, j: (0, 0)),
            pl.BlockSpec((1, r, D_MODEL), lambda i, j: (i * tm // rows_per_mod, 0, 0)),
            pl.BlockSpec((1, r, D_MODEL), lambda i, j: (i * tm // rows_per_mod, 0, 1)),
            pl.BlockSpec((D_MODEL, tn), lambda i, j: (0, j)),
        ],
        out_specs=pl.BlockSpec((tm, tn), lambda i, j: (i, j)),
        out_shape=jax.ShapeDtypeStruct((t, n), F32),
        scratch_shapes=[pltpu.VMEM((tm, D_MODEL), BF16)],
        compiler_params=_cparams(("parallel", "arbitrary"), 56),
        name="in_proj",
    )(x, nw.reshape(1, D_MODEL), mods, mods, w)


def _out_kernel(a_ref, w_ref, x_ref, g_ref, o_ref):
    o_ref[...] = x_ref[...] + g_ref[0] * _dot(a_ref[...].astype(BF16), w_ref[...])


def _out_call(a, w, x, mods, tm, tn, rows_per_mod):
    t, k = a.shape
    r = mods.shape[1]
    nj = D_MODEL // tn
    return pl.pallas_call(
        _out_kernel,
        grid=(t // tm, nj),
        in_specs=[
            pl.BlockSpec((tm, k), lambda i, j: (i, 0)),
            pl.BlockSpec((k, tn), lambda i, j: (0, j)),
            pl.BlockSpec((tm, tn), lambda i, j: (i, j)),
            pl.BlockSpec((1, r, tn), lambda i, j: (i * tm // rows_per_mod, 0, 2 * nj + j)),
        ],
        out_specs=pl.BlockSpec((tm, tn), lambda i, j: (i, j)),
        out_shape=jax.ShapeDtypeStruct((t, D_MODEL), F32),
        compiler_params=_cparams(("parallel", "parallel"), 56),
        name="out_proj",
    )(a, w, x, mods)


def _norm_kernel(x_ref, w_ref, o_ref):
    o_ref[...] = _rms(x_ref[...], w_ref[...])


def _norm_call(x, w, tm):
    t = x.shape[0]
    return pl.pallas_call(
        _norm_kernel,
        grid=(t // tm,),
        in_specs=[pl.BlockSpec((tm, D_MODEL), lambda i: (i, 0)), pl.BlockSpec((1, D_MODEL), lambda i: (0, 0))],
        out_specs=pl.BlockSpec((tm, D_MODEL), lambda i: (i, 0)),
        out_shape=jax.ShapeDtypeStruct((t, D_MODEL), F32),
        compiler_params=_cparams(("parallel",), 40),
        name="final_norm",
    )(x, w.reshape(1, D_MODEL))


def _rope(x, cos, ssin):
    lane = lax.broadcasted_iota(jnp.int32, x.shape, 1)
    first = (lane % A_ROPE) < (A_ROPE // 2)
    partner = jnp.where(first, pltpu.roll(x, LANES - A_ROPE // 2, 1), pltpu.roll(x, A_ROPE // 2, 1))
    return x * cos + partner * ssin


def _qkv_kernel(cq_ref, ckv_ref, kpe_ref, cos_ref, sin_ref, qn_ref, kvn_ref, wuq_ref, wk_ref,
                qcat_ref, kcat_ref, ckv_o, kpe_o):
    cos, ssin = cos_ref[...], sin_ref[...]
    ckv = _rms(ckv_ref[...], kvn_ref[...])
    ckv_o[...] = ckv
    kpe = _rope(kpe_ref[...], cos, ssin)
    kpe_o[...] = kpe
    kcat_ref[:, 0:A_KV_LORA] = ckv.astype(BF16)
    kcat_ref[:, A_KV_LORA:A_QK] = kpe.astype(BF16)

    q = _dot(_rms(cq_ref[...], qn_ref[...]).astype(BF16), wuq_ref[...])
    for h in range(A_HEADS):
        qh = q[:, h * A_NOPE:(h + 1) * A_NOPE].astype(BF16)
        qcat_ref[h, :, 0:A_KV_LORA] = (_dot(qh, wk_ref[h]) * A_SCALE).astype(qcat_ref.dtype)
    pe0 = A_HEADS * A_NOPE
    for pr in range(A_HEADS // 2):
        qp = _rope(q[:, pe0 + pr * LANES:pe0 + (pr + 1) * LANES], cos, ssin) * A_SCALE
        qcat_ref[2 * pr, :, A_KV_LORA:A_QK] = qp.astype(qcat_ref.dtype)
        qcat_ref[2 * pr + 1, :, A_KV_LORA:A_QK] = pltpu.roll(qp, A_ROPE, 1).astype(qcat_ref.dtype)


def _qkv_call(proj, cos, ssin, qn, kvn, wuq, wk, tm, q_dtype=BF16):
    t = proj.shape[0]
    cq_blk = D_MODEL // A_Q_LORA
    return pl.pallas_call(
        _qkv_kernel,
        grid=(t // tm,),
        in_specs=[
            pl.BlockSpec((tm, A_Q_LORA), lambda i: (i, cq_blk)),
            pl.BlockSpec((tm, A_KV_LORA), lambda i: (i, cq_blk + 1)),
            pl.BlockSpec((tm, LANES), lambda i: (i, (D_MODEL + A_Q_LORA + A_KV_LORA) // LANES)),
            pl.BlockSpec((tm, LANES), lambda i: (i, 0)),
            pl.BlockSpec((tm, LANES), lambda i: (i, 0)),
            pl.BlockSpec((1, A_Q_LORA), lambda i: (0, 0)),
            pl.BlockSpec((1, A_KV_LORA), lambda i: (0, 0)),
            pl.BlockSpec(wuq.shape, lambda i: (0, 0)),
            pl.BlockSpec(wk.shape, lambda i: (0, 0, 0)),
        ],
        out_specs=[
            pl.BlockSpec((A_HEADS, tm, A_QK), lambda i: (0, i, 0)),
            pl.BlockSpec((tm, A_QK), lambda i: (i, 0)),
            pl.BlockSpec((tm, A_KV_LORA), lambda i: (i, 0)),
            pl.BlockSpec((tm, LANES), lambda i: (i, 0)),
        ],
        out_shape=[
            jax.ShapeDtypeStruct((A_HEADS, t, A_QK), q_dtype),
            jax.ShapeDtypeStruct((t, A_QK), BF16),
            jax.ShapeDtypeStruct((t, A_KV_LORA), F32),
            jax.ShapeDtypeStruct((t, LANES), F32),
        ],
        compiler_params=_cparams(("parallel",), 48),
        name="mla_qkv",
    )(proj, proj, proj, cos, ssin, qn.reshape(1, -1), kvn.reshape(1, -1), wuq, wk)


def _attn_prompt_kernel(q_ref, k_ref, gate_ref, wv_ref, o_ref, *scratch, tq, tk, nk, hg):
    i, j = pl.program_id(1), pl.program_id(2)
    groups = A_HEADS // hg
    m_refs, l_refs, acc_refs = scratch[:groups], scratch[groups:2 * groups], scratch[2 * groups:]

    @pl.when(j == 0)
    def _():
        for g in range(groups):
            m_refs[g][...] = jnp.full(m_refs[g].shape, NEG, F32)
            l_refs[g][...] = jnp.zeros(l_refs[g].shape, F32)
            acc_refs[g][...] = jnp.zeros(acc_refs[g].shape, F32)

    def step(masked):
        k = k_ref[...]
        v = k[:, 0:A_KV_LORA]
        def scores(g):
            return _dot_nt(q_ref[g * hg:(g + 1) * hg].reshape(hg * tq, A_QK), k)

        s_next = scores(0)
        for g in range(groups):
            s = s_next
            if g + 1 < groups:
                s_next = scores(g + 1)
            if masked:
                qpos = i * tq + lax.broadcasted_iota(jnp.int32, (hg, tq, tk), 1).reshape(hg * tq, tk)
                kpos = j * tk + lax.broadcasted_iota(jnp.int32, (hg * tq, tk), 1)
                s = jnp.where(kpos <= qpos, s, NEG)
            m_old = m_refs[g][...]
            m_new = jnp.maximum(m_old, jnp.max(s, axis=-1, keepdims=True))
            p = jnp.exp(s - m_new)
            alpha = jnp.exp(m_old - m_new)
            l_refs[g][...] = alpha * l_refs[g][...] + jnp.sum(p, axis=-1, keepdims=True)
            acc_refs[g][...] = alpha * acc_refs[g][...] + _dot(p.astype(BF16), v)
            m_refs[g][...] = m_new

    first_q, last_q = i * tq, i * tq + tq - 1
    pl.when(j * tk + tk - 1 <= first_q)(lambda: step(False))
    pl.when((j * tk <= last_q) & (j * tk + tk - 1 > first_q))(lambda: step(True))

    @pl.when(j == nk - 1)
    def _():
        for h in range(A_HEADS):
            g, sl = h // hg, slice((h % hg) * tq, (h % hg + 1) * tq)
            o_lat = (acc_refs[g][sl, :] / l_refs[g][sl, :]).astype(BF16)
            cols = slice(h * A_VDIM, (h + 1) * A_VDIM)
            o_ref[:, cols] = (_dot(o_lat, wv_ref[h]) * _silu(gate_ref[:, cols])).astype(BF16)


def _attn_prompt_call(qcat, kcat, proj, wv, bsz, seq, tq, tk, hg=A_HEADS):
    nq, nk = seq // tq, seq // tk
    kern = functools.partial(_attn_prompt_kernel, tq=tq, tk=tk, nk=nk, hg=hg)
    groups, rows = A_HEADS // hg, hg * tq
    scratch = ([pltpu.VMEM((rows, 1), F32)] * (2 * groups)) + ([pltpu.VMEM((rows, A_KV_LORA), F32)] * groups)
    return pl.pallas_call(
        kern,
        grid=(bsz, nq, nk),
        in_specs=[
            pl.BlockSpec((A_HEADS, tq, A_QK), lambda b, i, j: (0, b * nq + i, 0)),
            pl.BlockSpec((tk, A_QK), lambda b, i, j: (b * nk + jnp.minimum(j, (i * tq + tq - 1) // tk), 0)),
            pl.BlockSpec((tq, A_WIDTH), lambda b, i, j: (b * nq + i, 0)),
            pl.BlockSpec(wv.shape, lambda b, i, j: (0, 0, 0)),
        ],
        out_specs=pl.BlockSpec((tq, A_WIDTH), lambda b, i, j: (b * nq + i, 0)),
        out_shape=jax.ShapeDtypeStruct((bsz * seq, A_WIDTH), BF16),
        scratch_shapes=scratch,
        compiler_params=_cparams(("parallel", "parallel", "arbitrary"), 56),
        name="mla_attn_prompt",
    )(qcat, kcat, proj, wv)


def _attn_sample_kernel(pt_ref, q_ref, ckvn_ref, kpen_ref, *refs, pages, nsteps, lq, chains):
    ckv_refs, kpe_refs = refs[:pages], refs[pages:2 * pages]
    o_ref, m_ref, l_ref, acc_ref, kbuf, pbuf, qbuf, knbuf = refs[2 * pages:]
    b, c = pl.program_id(0), pl.program_id(1)
    rows = qbuf.shape[0]
    toks = [pl.ds((b % SUBLANES) * lq + t, 1) for t in range(lq)]

    @pl.when(c == 0)
    def _():
        m_ref[...] = jnp.full(m_ref.shape, NEG, F32)
        l_ref[...] = jnp.zeros(l_ref.shape, F32)
        acc_ref[...] = jnp.zeros(acc_ref.shape, F32)
        qbuf[...] = jnp.zeros(qbuf.shape, F32)
        knbuf[...] = jnp.zeros(knbuf.shape, F32)
        for t, tok in enumerate(toks):
            for h in range(A_HEADS):
                qbuf[h * lq + t:h * lq + t + 1, :] = q_ref[h, tok, :]
            knbuf[t:t + 1, 0:A_KV_LORA] = ckvn_ref[tok, :]
            knbuf[t:t + 1, A_KV_LORA:A_QK] = kpen_ref[tok, :]

    q = qbuf[...].astype(BF16)
    ql, qp = q[:, 0:A_KV_LORA], q[:, A_KV_LORA:A_KV_LORA + A_ROPE]

    def as_col(row):
        return jnp.broadcast_to(row, (rows, rows)).T[:, 0:1]

    def local(st, vals):
        m = jnp.max(st, axis=0, keepdims=True)
        pt = jnp.exp(st - m)
        return m, jnp.sum(pt, axis=0, keepdims=True), _dot_tn(pt.astype(BF16), vals)

    def merge(parts):
        m_old = m_ref[...]
        m_new = m_old
        for m, _, _ in parts:
            m_new = jnp.maximum(m_new, m)
        alpha = jnp.exp(m_old - m_new)
        l_new = alpha * l_ref[...]
        acc = as_col(alpha) * acc_ref[...]
        for m, l, o in parts:
            w = jnp.exp(m - m_new)
            l_new = l_new + w * l
            acc = acc + as_col(w) * o
        m_ref[...], l_ref[...], acc_ref[...] = m_new, l_new, acc

    per = pages // chains

    def scores(ch):
        for n in range(ch * per, (ch + 1) * per):
            kbuf[n * PAGE_SIZE:(n + 1) * PAGE_SIZE, :] = ckv_refs[n][0, 0].astype(BF16)
            pbuf[n * PAGE_SIZE:(n + 1) * PAGE_SIZE, :] = kpe_refs[n][0, 0].T.astype(BF16)
        sl = slice(ch * per * PAGE_SIZE, (ch + 1) * per * PAGE_SIZE)
        keys = kbuf[sl, :]
        return _dot_nt(keys, ql) + _dot_nt(pbuf[sl, :], qp), keys

    parts = []
    nxt = scores(0)
    for ch in range(chains):
        st, keys = nxt
        if ch + 1 < chains:
            nxt = scores(ch + 1)
        parts.append(local(st, keys))
    merge(parts)

    @pl.when(c == nsteps - 1)
    def _():
        kn = knbuf[...].astype(BF16)
        t_k = lax.broadcasted_iota(jnp.int32, (PAGE_SIZE, rows), 0)
        t_q = lax.broadcasted_iota(jnp.int32, (PAGE_SIZE, rows), 1) % lq
        merge([local(jnp.where(t_k <= t_q, _dot_nt(kn, q), NEG), kn[:, 0:A_KV_LORA])])
        o = acc_ref[...] / as_col(l_ref[...])
        for h in range(A_HEADS):
            for t, tok in enumerate(toks):
                o_ref[h, tok, :] = o[h * lq + t:h * lq + t + 1, :]


def _attn_sample_call(page_table, qcat, ckv_new, kpe_new, cache_ckv, cache_kpe_t, layer, lq, pages, chains=2):
    nb, n_pages = page_table.shape
    nsteps = n_pages // pages
    rows = -(-A_HEADS * lq // LANES) * LANES
    blk = SUBLANES * lq
    kern = functools.partial(_attn_sample_kernel, pages=pages, nsteps=nsteps, lq=lq, chains=chains)

    def page_map(n):
        return lambda b, c, pt: (layer, pt[b, c * pages + n], 0, 0)

    in_specs = [
        pl.BlockSpec((A_HEADS, blk, A_QK), lambda b, c, pt: (0, b // SUBLANES, 0)),
        pl.BlockSpec((blk, A_KV_LORA), lambda b, c, pt: (b // SUBLANES, 0)),
        pl.BlockSpec((blk, LANES), lambda b, c, pt: (b // SUBLANES, 0)),
    ]
    in_specs += [pl.BlockSpec((1, 1, PAGE_SIZE, A_KV_LORA), page_map(n)) for n in range(pages)]
    in_specs += [pl.BlockSpec((1, 1, A_ROPE, PAGE_SIZE), page_map(n)) for n in range(pages)]
    grid_spec = pltpu.PrefetchScalarGridSpec(
        num_scalar_prefetch=1,
        grid=(nb, nsteps),
        in_specs=in_specs,
        out_specs=pl.BlockSpec((A_HEADS, blk, A_KV_LORA), lambda b, c, pt: (0, b // SUBLANES, 0)),
        scratch_shapes=[pltpu.VMEM((1, rows), F32), pltpu.VMEM((1, rows), F32), pltpu.VMEM((rows, A_KV_LORA), F32),
                        pltpu.VMEM((pages * PAGE_SIZE, A_KV_LORA), BF16), pltpu.VMEM((pages * PAGE_SIZE, A_ROPE), BF16),
                        pltpu.VMEM((rows, A_QK), F32), pltpu.VMEM((PAGE_SIZE, A_QK), F32)],
    )
    return pl.pallas_call(
        kern,
        grid_spec=grid_spec,
        out_shape=jax.ShapeDtypeStruct((A_HEADS, nb * lq, A_KV_LORA), F32),
        compiler_params=_cparams(("arbitrary", "arbitrary"), 48),
        name="mla_attn_sample",
    )(page_table, qcat, ckv_new, kpe_new, *([cache_ckv] * pages), *([cache_kpe_t] * pages))


def _oproj_kernel(o_ref, wv_ref, gate_ref, out_ref):
    out_ref[...] = (_dot(o_ref[0].astype(BF16), wv_ref[0]) * _silu(gate_ref[...])).astype(BF16)


def _oproj_call(o_lat, wv, proj):
    t = o_lat.shape[1]
    return pl.pallas_call(
        _oproj_kernel,
        grid=(A_HEADS,),
        in_specs=[
            pl.BlockSpec((1, t, A_KV_LORA), lambda h: (h, 0, 0)),
            pl.BlockSpec((1, A_KV_LORA, A_VDIM), lambda h: (h, 0, 0)),
            pl.BlockSpec((t, A_VDIM), lambda h: (0, h)),
        ],
        out_specs=pl.BlockSpec((t, A_VDIM), lambda h: (0, h)),
        out_shape=jax.ShapeDtypeStruct((t, A_WIDTH), BF16),
        compiler_params=_cparams(("parallel",), 32),
        name="mla_oproj_sample",
    )(o_lat, wv, proj)


def _group_norm(y, nw):
    outs = []
    for g in range(B_GROUPS):
        sl = slice(g * B_GW, (g + 1) * B_GW)
        outs.append(_rms(y[:, sl], nw[:, sl]))
    return jnp.concatenate(outs, axis=1)


def _ssd_prompt_kernel(proj_ref, cw_ref, cb_ref, dtb_ref, alog_ref, dexp_ref, nw_ref, e_ref, tril_ref,
                       y_ref, tail_ref, hlast_ref, xp_ref, h_ref, yacc_ref, *, nchunks):
    c = pl.program_id(1)
    q = CHUNK

    @pl.when(c == 0)
    def _():
        xp_ref[0:SUBLANES, :] = jnp.zeros((SUBLANES, B_CONV_DIM), F32)
        h_ref[...] = jnp.zeros(h_ref.shape, F32)

    raw = proj_ref[:, B_WIDTH:B_WIDTH + B_CONV_DIM]
    xp_ref[SUBLANES:SUBLANES + q, :] = raw
    conv = cb_ref[...] + cw_ref[3:4, :] * raw
    for back in range(1, B_CONV):
        conv = conv + cw_ref[3 - back:4 - back, :] * xp_ref[SUBLANES - back:SUBLANES - back + q, :]
    xp_ref[0:SUBLANES, :] = raw[q - SUBLANES:q, :]
    tail_ref[0] = raw[q - SUBLANES:q, :]
    xc = _silu(conv)
    x = xc[:, 0:B_WIDTH]
    bm = xc[:, B_WIDTH:B_WIDTH + B_GN].astype(BF16)
    cm = xc[:, B_WIDTH + B_GN:B_CONV_DIM].astype(BF16)

    dt = _softplus(proj_ref[:, B_WIDTH + B_CONV_DIM:B_IN_PAD] + dtb_ref[...])
    dta = dt * (-jnp.exp(alog_ref[...]))
    acum = _split_dot(dta, tril_ref[...], 3, left=True)
    acum_t = acum.T
    alast = acum[q - 1:q, :]
    e = e_ref[...]
    dtx = _split_dot(dt, e, 2)
    eax = _split_dot(jnp.exp(acum), e, 2)
    wx = _split_dot(jnp.exp(alast - acum) * dt, e, 2)
    lane = lax.broadcasted_iota(jnp.int32, (q, B_WIDTH), 1)
    even = (lane % LANES) < B_HEADDIM
    xdt = x * dtx
    xdt_even = jnp.where(even, xdt, 0.0).astype(BF16)
    xdt_odd = jnp.where(even, 0.0, xdt).astype(BF16)
    xw = (x * wx).astype(BF16)
    causal = lax.broadcasted_iota(jnp.int32, (q, q), 0) >= lax.broadcasted_iota(jnp.int32, (q, q), 1)
    dec_col = jnp.exp(acum_t[:, q - 1:q])

    for g in range(B_GROUPS):
        cg = cm[:, g * B_STATE:(g + 1) * B_STATE]
        bg = bm[:, g * B_STATE:(g + 1) * B_STATE]
        cb = _dot_nt(cg, bg)
        hg = h_ref[g * B_HPG:(g + 1) * B_HPG].reshape(B_GW, B_STATE)
        ystate = _dot_nt(cg, hg.astype(BF16))
        for pr in range(B_HPG // 2):
            ls = []
            for hh in (2 * pr, 2 * pr + 1):
                h = g * B_HPG + hh
                diff = acum[:, h:h + 1] - acum_t[h:h + 1, :]
                ls.append((cb * jnp.exp(jnp.where(causal, diff, NEG))).astype(BF16))
            col = g * B_GW + pr * LANES
            lhs = jnp.concatenate(ls, axis=1)
            rhs = jnp.concatenate([xdt_even[:, col:col + LANES], xdt_odd[:, col:col + LANES]], axis=0)
            yacc_ref[:, col:col + LANES] = _dot(lhs, rhs) + ystate[:, pr * LANES:(pr + 1) * LANES] * eax[:, col:col + LANES]
        snew = _dot_tn(xw[:, g * B_GW:(g + 1) * B_GW], bg)
        for hh in range(B_HPG):
            h = g * B_HPG + hh
            dec = jnp.broadcast_to(dec_col[h:h + 1, :], (B_HEADDIM, B_STATE))
            h_ref[h] = h_ref[h] * dec + snew[hh * B_HEADDIM:(hh + 1) * B_HEADDIM, :]

    y = (yacc_ref[...] + dexp_ref[...] * x) * _silu(proj_ref[:, 0:B_WIDTH])
    y_ref[...] = _group_norm(y, nw_ref[...]).astype(BF16)

    @pl.when(c == nchunks - 1)
    def _():
        hlast_ref[0] = h_ref[...]


def _ssd_prompt_call(proj, cw, cb, dtb, alog, dexp, nw, e, tril, bsz, seq):
    nchunks = seq // CHUNK
    kern = functools.partial(_ssd_prompt_kernel, nchunks=nchunks)
    const = lambda b, c: (0, 0)
    return pl.pallas_call(
        kern,
        grid=(bsz, nchunks),
        in_specs=[
            pl.BlockSpec((CHUNK, B_IN_PAD), lambda b, c: (b * nchunks + c, 0)),
            pl.BlockSpec((B_CONV, B_CONV_DIM), const),
            pl.BlockSpec((1, B_CONV_DIM), const),
            pl.BlockSpec((1, LANES), const),
            pl.BlockSpec((1, LANES), const),
            pl.BlockSpec((1, B_WIDTH), const),
            pl.BlockSpec((1, B_WIDTH), const),
            pl.BlockSpec((LANES, B_WIDTH), const),
            pl.BlockSpec((CHUNK, CHUNK), const),
        ],
        out_specs=[
            pl.BlockSpec((CHUNK, B_WIDTH), lambda b, c: (b * nchunks + c, 0)),
            pl.BlockSpec((1, SUBLANES, B_CONV_DIM), lambda b, c: (b, 0, 0)),
            pl.BlockSpec((1, B_HEADS, B_HEADDIM, B_STATE), lambda b, c: (b, 0, 0, 0)),
        ],
        out_shape=[
            jax.ShapeDtypeStruct((bsz * seq, B_WIDTH), BF16),
            jax.ShapeDtypeStruct((bsz, SUBLANES, B_CONV_DIM), F32),
            jax.ShapeDtypeStruct((bsz, B_HEADS, B_HEADDIM, B_STATE), F32),
        ],
        scratch_shapes=[
            pltpu.VMEM((SUBLANES + CHUNK, B_CONV_DIM), F32),
            pltpu.VMEM((B_HEADS, B_HEADDIM, B_STATE), F32),
            pltpu.VMEM((CHUNK, B_WIDTH), F32),
        ],
        compiler_params=_cparams(("parallel", "arbitrary"), 56),
        name="ssd_prompt",
    )(proj, cw, cb, dtb, alog, dexp, nw, e, tril)


def _ssd_sample_conv_kernel(x_ref, prev_ref, cw_ref, cb_ref, o_ref, *, lq):
    rows = [prev_ref[n] for n in range(B_CONV - 1)] + [x_ref[n] for n in range(lq)]
    for t in range(lq):
        conv = cb_ref[...]
        for j in range(B_CONV):
            conv = conv + cw_ref[j:j + 1, :] * rows[t + j]
        o_ref[t] = _silu(conv)


def _ssd_sample_conv_call(proj_t, prev_t, cw, cb):
    lq, nb = proj_t.shape[:2]
    tc = 2048
    off = B_WIDTH // tc
    kern = functools.partial(_ssd_sample_conv_kernel, lq=lq)
    return pl.pallas_call(
        kern,
        grid=(B_CONV_DIM // tc,),
        in_specs=[
            pl.BlockSpec((lq, nb, tc), lambda j: (0, 0, off + j)),
            pl.BlockSpec((B_CONV - 1, nb, tc), lambda j: (0, 0, j)),
            pl.BlockSpec((B_CONV, tc), lambda j: (0, j)),
            pl.BlockSpec((1, tc), lambda j: (0, j)),
        ],
        out_specs=pl.BlockSpec((lq, nb, tc), lambda j: (0, 0, j)),
        out_shape=jax.ShapeDtypeStruct((lq, nb, B_CONV_DIM), F32),
        compiler_params=_cparams(("parallel",), 40),
        name="ssd_sample_conv",
    )(proj_t, prev_t, cw, cb)


def _ssd_sample_intra_kernel(xc_ref, dt_ref, dtb_ref, alog_ref, dexp_ref, r_ref, e_ref,
                             yin_ref, eax_ref, xw_ref, dec_ref, *, lq):
    qi = pl.program_id(0)
    a = -jnp.exp(alog_ref[...])
    dts = [_softplus(dt_ref[k] + dtb_ref[...]) for k in range(lq)]
    acums = []
    run = None
    for k in range(lq):
        run = dts[k] * a if run is None else run + dts[k] * a
        acums.append(run)
    acum_q = acums[0]
    for k in range(1, lq):
        acum_q = jnp.where(qi >= k, acums[k], acum_q)
    e, r = e_ref[...], r_ref[...]
    x_q = xc_ref[qi, :, 0:B_WIDTH]
    c_q = xc_ref[qi, :, B_WIDTH + B_GN:B_CONV_DIM]
    y = dexp_ref[...] * x_q
    for k in range(lq):
        b_k = xc_ref[k, :, B_WIDTH:B_WIDTH + B_GN]
        cb = _split_dot(c_q * b_k, r, 2)
        m = cb * jnp.exp(jnp.where(qi >= k, acum_q - acums[k], NEG)) * dts[k]
        y = y + _split_dot(m, e, 2) * xc_ref[k, :, 0:B_WIDTH]
    yin_ref[0] = y
    eax_ref[0] = _split_dot(jnp.exp(acum_q), e, 2)
    dt_q = dts[0]
    for k in range(1, lq):
        dt_q = jnp.where(qi == k, dts[k], dt_q)
    xw_ref[0] = x_q * _split_dot(jnp.exp(acums[lq - 1] - acum_q) * dt_q, e, 2)
    dec_ref[...] = jnp.exp(acums[lq - 1])


def _ssd_sample_intra_call(xc, proj_t, dtb, alog, dexp, r, e):
    lq, nb = xc.shape[:2]
    kern = functools.partial(_ssd_sample_intra_kernel, lq=lq)
    tok = lambda q: (q, 0, 0)
    const = lambda q: (0, 0)
    big = jax.ShapeDtypeStruct((lq, nb, B_WIDTH), F32)
    return pl.pallas_call(
        kern,
        grid=(lq,),
        in_specs=[
            pl.BlockSpec((lq, nb, B_CONV_DIM), lambda q: (0, 0, 0)),
            pl.BlockSpec((lq, nb, LANES), lambda q: (0, 0, (B_WIDTH + B_CONV_DIM) // LANES)),
            pl.BlockSpec((1, LANES), const),
            pl.BlockSpec((1, LANES), const),
            pl.BlockSpec((1, B_WIDTH), const),
            pl.BlockSpec((B_GN, LANES), const),
            pl.BlockSpec((LANES, B_WIDTH), const),
        ],
        out_specs=[
            pl.BlockSpec((1, nb, B_WIDTH), tok),
            pl.BlockSpec((1, nb, B_WIDTH), tok),
            pl.BlockSpec((1, nb, B_WIDTH), tok),
            pl.BlockSpec((nb, LANES), const),
        ],
        out_shape=[big, big, big, jax.ShapeDtypeStruct((nb, LANES), F32)],
        compiler_params=_cparams(("arbitrary",), 56),
        name="ssd_sample_intra",
    )(xc, proj_t, dtb, alog, dexp, r, e)


def _ssd_sample_state_kernel(dec_ref, h0_ref, c_ref, b_ref, xw_ref, yin_ref, eax_ref, z_ref, nw_ref,
                             y_ref, hl_ref, c8, b8, xw8, *, lq):
    s = pl.program_id(0)
    r = pl.ds(s % SUBLANES, 1)
    for buf in (c8, b8, xw8):
        buf[lq:SUBLANES, :] = jnp.zeros((SUBLANES - lq, buf.shape[1]), F32)
    for t in range(lq):
        c8[t:t + 1, :] = c_ref[t, r, :]
        b8[t:t + 1, :] = b_ref[t, r, :]
        xw8[t:t + 1, :] = xw_ref[t, r, :]
    ys = []
    for g in range(B_GROUPS):
        hg = h0_ref[0, g * B_HPG:(g + 1) * B_HPG].reshape(B_GW, B_STATE).astype(BF16)
        cg = c8[:, g * B_STATE:(g + 1) * B_STATE].astype(BF16)
        bg = b8[:, g * B_STATE:(g + 1) * B_STATE].astype(BF16)
        ys.append(_dot_nt(cg, hg))
        snew = _dot_tn(xw8[:, g * B_GW:(g + 1) * B_GW].astype(BF16), bg)
        for hh in range(B_HPG):
            h = g * B_HPG + hh
            hl_ref[0, h] = h0_ref[0, h] * dec_ref[s, h] + snew[hh * B_HEADDIM:(hh + 1) * B_HEADDIM, :]
    ystate = jnp.concatenate(ys, axis=1)
    for t in range(lq):
        y = (yin_ref[t, r, :] + ystate[t:t + 1, :] * eax_ref[t, r, :]) * _silu(z_ref[t, r, :])
        y_ref[0, t:t + 1, :] = _group_norm(y, nw_ref[...])


def _ssd_sample_state_call(dec, h0, xc, xw, yin, eax, proj_t, nw):
    lq, nb = xc.shape[:2]
    kern = functools.partial(_ssd_sample_state_kernel, lq=lq)
    row = lambda blk: (lambda s, d: (0, s // SUBLANES, blk))
    grid_spec = pltpu.PrefetchScalarGridSpec(
        num_scalar_prefetch=1,
        grid=(nb,),
        in_specs=[
            pl.BlockSpec((1, B_HEADS, B_HEADDIM, B_STATE), lambda s, d: (s, 0, 0, 0)),
            pl.BlockSpec((lq, SUBLANES, B_GN), row((B_WIDTH + B_GN) // B_GN)),
            pl.BlockSpec((lq, SUBLANES, B_GN), row(B_WIDTH // B_GN)),
            pl.BlockSpec((lq, SUBLANES, B_WIDTH), row(0)),
            pl.BlockSpec((lq, SUBLANES, B_WIDTH), row(0)),
            pl.BlockSpec((lq, SUBLANES, B_WIDTH), row(0)),
            pl.BlockSpec((lq, SUBLANES, B_WIDTH), row(0)),
            pl.BlockSpec((1, B_WIDTH), lambda s, d: (0, 0)),
        ],
        out_specs=[
            pl.BlockSpec((1, lq, B_WIDTH), lambda s, d: (s, 0, 0)),
            pl.BlockSpec((1, B_HEADS, B_HEADDIM, B_STATE), lambda s, d: (s, 0, 0, 0)),
        ],
        scratch_shapes=[pltpu.VMEM((SUBLANES, B_GN), F32), pltpu.VMEM((SUBLANES, B_GN), F32),
                        pltpu.VMEM((SUBLANES, B_WIDTH), F32)],
    )
    return pl.pallas_call(
        kern,
        grid_spec=grid_spec,
        out_shape=[jax.ShapeDtypeStruct((nb, lq, B_WIDTH), F32),
                   jax.ShapeDtypeStruct((nb, B_HEADS, B_HEADDIM, B_STATE), F32)],
        compiler_params=_cparams(("arbitrary",), 40),
        name="ssd_sample_state",
    )(dec, h0, xc, xc, xw, yin, eax, proj_t, nw)


def _gmlp_kernel(p_ref, wm_ref, bias_ref, lng_ref, lnb_ref, o_ref, *v_refs):
    q = CHUNK
    u = _gelu(p_ref[:, 0:C_WIDTH])
    vp = _gelu(p_ref[:, C_WIDTH:2 * C_WIDTH])
    mu = jnp.mean(vp, axis=-1, keepdims=True)
    var = jnp.mean(jnp.square(vp - mu), axis=-1, keepdims=True)
    v = (vp - mu) * lax.rsqrt(var + EPS) * lng_ref[...] + lnb_ref[...]
    if v_refs:
        v_refs[0][...] = v
    vb = v.astype(BF16)
    causal = lax.broadcasted_iota(jnp.int32, (q, q), 0) >= lax.broadcasted_iota(jnp.int32, (q, q), 1)
    for g in range(C_GROUPS):
        sl = slice(g * C_GDIM, (g + 1) * C_GDIM)
        w = jnp.where(causal, wm_ref[g], 0.0).astype(BF16)
        mix = _dot(w, vb[:, sl]) + bias_ref[:, sl]
        gate = p_ref[:, 2 * C_WIDTH + g * C_GDIM:2 * C_WIDTH + (g + 1) * C_GDIM]
        o_ref[:, sl] = (u[:, sl] * mix * _silu(gate)).astype(BF16)


def _gmlp_call(proj, wm, bias, lng, lnb, want_v):
    t = proj.shape[0]
    row = lambda i: (i, 0)
    const = lambda i: (0, 0)
    out_specs = [pl.BlockSpec((CHUNK, C_WIDTH), row)]
    out_shape = [jax.ShapeDtypeStruct((t, C_WIDTH), BF16)]
    if want_v:
        out_specs.append(pl.BlockSpec((CHUNK, C_WIDTH), row))
        out_shape.append(jax.ShapeDtypeStruct((t, C_WIDTH), F32))
    return pl.pallas_call(
        _gmlp_kernel,
        grid=(t // CHUNK,),
        in_specs=[
            pl.BlockSpec((CHUNK, 3 * C_WIDTH), row),
            pl.BlockSpec((C_GROUPS, CHUNK, CHUNK), lambda i: (0, 0, 0)),
            pl.BlockSpec((CHUNK, C_WIDTH), const),
            pl.BlockSpec((1, C_WIDTH), const),
            pl.BlockSpec((1, C_WIDTH), const),
        ],
        out_specs=out_specs,
        out_shape=out_shape,
        compiler_params=_cparams(("parallel",), 48),
        name="gmlp_mix",
    )(proj, wm, bias, lng.reshape(1, C_WIDTH), lnb.reshape(1, C_WIDTH))


def _pad_cols(w, n):
    return jnp.pad(w, ((0, 0), (0, n - w.shape[1])))


def _rope_tables(pos):
    half = A_ROPE // 2
    inv = ROPE_THETA ** (-jnp.arange(half, dtype=F32) / half)
    ang = pos.astype(F32)[:, None] * inv[None, :]
    cos, sin = jnp.cos(ang), jnp.sin(ang)
    return jnp.tile(cos, (1, LANES // half)), jnp.tile(jnp.concatenate([-sin, sin], axis=1), (1, LANES // A_ROPE))


def _head_expand():
    rows = lax.broadcasted_iota(jnp.int32, (LANES, B_WIDTH), 0)
    cols = lax.broadcasted_iota(jnp.int32, (LANES, B_WIDTH), 1)
    return (cols // B_HEADDIM == rows).astype(BF16)


def _group_reduce():
    rows = lax.broadcasted_iota(jnp.int32, (B_GN, LANES), 0)
    cols = lax.broadcasted_iota(jnp.int32, (B_GN, LANES), 1)
    return ((rows // B_STATE == cols // B_HPG) & (cols < B_HEADS)).astype(BF16)


def kernel(x_prompt, x_sample, c_prompt, c_sample, cache_ckv, cache_kpe, page_table, state_ssm, state_conv, ada_w, ada_b, norm_w, norm_f, a_w_in, a_q_norm, a_w_uq, a_kv_norm, a_w_ukv, a_w_out, b_w_in, b_conv_w, b_conv_b, b_dt_bias, b_a_log, b_d, b_norm, b_w_out, c_w_in, c_ln_g, c_ln_b, c_ws, c_bs, c_w_out):
    bsz, seq, _ = x_prompt.shape
    nb, lq, _ = x_sample.shape
    tp, ts = bsz * seq, nb * lq
    past_len = page_table.shape[1] * PAGE_SIZE

    mods = _ada_call(jnp.concatenate([c_prompt, c_sample], axis=0), ada_w, ada_b)
    e_mat, r_mat = _head_expand(), _group_reduce()
    tril = (lax.broadcasted_iota(jnp.int32, (CHUNK, CHUNK), 0)
            >= lax.broadcasted_iota(jnp.int32, (CHUNK, CHUNK), 1)).astype(BF16)
    cos_p, sin_p = _rope_tables(jnp.tile(jnp.arange(seq, dtype=jnp.int32), bsz))
    cos_s, sin_s = _rope_tables(jnp.tile(past_len + jnp.arange(lq, dtype=jnp.int32), nb))

    cache_kpe_t = cache_kpe.transpose(0, 1, 3, 2)
    xp = x_prompt.reshape(tp, D_MODEL)
    xs = x_sample.reshape(ts, D_MODEL)
    tm_p, tm_s = 1024, ts
    outs = dict(ckv_p=[], kpe_p=[], ckv_s=[], kpe_s=[], ssm_p=[], conv_p=[], ssm_s=[], conv_s=[], v_s=[])

    for l in range(DEPTH):
        kind, j = l % 3, l // 3
        mods_p = mods[l, :bsz].reshape(bsz, 1, 3 * D_MODEL)
        mods_s = jnp.repeat(mods[l, bsz:], lq, axis=0).reshape(1, ts, 3 * D_MODEL)
        if kind == 0:
            o1, o2, o3 = A_Q_LORA, A_Q_LORA + A_KV_LORA, A_Q_LORA + A_KV_LORA + A_ROPE
            w = a_w_in[j]
            w_in = _pad_cols(jnp.concatenate([w[:, o3:], w[:, :o3]], axis=1), A_IN_PAD).astype(BF16)
            wuq = a_w_uq[j].reshape(A_Q_LORA, A_HEADS, A_NOPE + A_ROPE)
            wuq = jnp.concatenate([wuq[:, :, :A_NOPE].reshape(A_Q_LORA, -1),
                                   wuq[:, :, A_NOPE:].reshape(A_Q_LORA, -1)], axis=1).astype(BF16)
            wk = a_w_ukv[j][:, :, :A_NOPE].transpose(1, 2, 0).astype(BF16)
            wv = a_w_ukv[j][:, :, A_NOPE:].transpose(1, 0, 2).astype(BF16)
            w_out = a_w_out[j].astype(BF16)

            proj_p = _in_call(xp, norm_w[l], mods_p, w_in, tm_p, 640, seq)
            qcat, kcat, ckv, kpe = _qkv_call(proj_p, cos_p, sin_p, a_q_norm[j], a_kv_norm[j], wuq, wk, 256)
            a_p = _attn_prompt_call(qcat, kcat, proj_p, wv, bsz, seq, 128, 512, 2)
            xp = _out_call(a_p, w_out, xp, mods_p, tm_p, 512, seq)
            outs["ckv_p"].append(ckv.reshape(bsz, seq, A_KV_LORA))
            outs["kpe_p"].append(kpe[:, :A_ROPE].reshape(bsz, seq, A_ROPE))

            proj_s = _in_call(xs, norm_w[l], mods_s, w_in, tm_s, 640, ts)
            qcat, kcat, ckv, kpe = _qkv_call(proj_s, cos_s, sin_s, a_q_norm[j], a_kv_norm[j], wuq, wk, 256, F32)
            o_lat = _attn_sample_call(page_table, qcat, ckv, kpe, cache_ckv, cache_kpe_t, j, lq, 32, 4)
            a_s = _oproj_call(o_lat, wv, proj_s)
            xs = _out_call(a_s, w_out, xs, mods_s, tm_s, 512, ts)
            outs["ckv_s"].append(ckv.reshape(nb, lq, A_KV_LORA))
            outs["kpe_s"].append(kpe[:, :A_ROPE].reshape(nb, lq, A_ROPE))
        elif kind == 1:
            w_in = _pad_cols(b_w_in[j], B_IN_PAD).astype(BF16)
            w_out = b_w_out[j].astype(BF16)
            cw, cb = b_conv_w[j], b_conv_b[j].reshape(1, B_CONV_DIM)
            dtb = jnp.pad(b_dt_bias[j], (0, LANES - B_HEADS)).reshape(1, LANES)
            alog = jnp.pad(b_a_log[j], (0, LANES - B_HEADS)).reshape(1, LANES)
            dexp = jnp.repeat(b_d[j], B_HEADDIM).reshape(1, B_WIDTH)
            nw = b_norm[j].reshape(1, B_WIDTH)

            proj_p = _in_call(xp, norm_w[l], mods_p, w_in, tm_p, 1152, seq)
            y_p, tail, h_last = _ssd_prompt_call(proj_p, cw, cb, dtb, alog, dexp, nw, e_mat, tril, bsz, seq)
            xp = _out_call(y_p, w_out, xp, mods_p, tm_p, 512, seq)
            outs["ssm_p"].append(h_last)
            outs["conv_p"].append(tail[:, SUBLANES - (B_CONV - 1):])

            proj_s = _in_call(xs, norm_w[l], mods_s, w_in, tm_s, 1152, ts)
            proj_t = proj_s.reshape(nb, lq, B_IN_PAD).transpose(1, 0, 2)
            prev_t = state_conv[j].transpose(1, 0, 2)
            xc = _ssd_sample_conv_call(proj_t, prev_t, cw, cb)
            yin, eax, xw, dec = _ssd_sample_intra_call(xc, proj_t, dtb, alog, dexp, r_mat, e_mat)
            y_s, h_last = _ssd_sample_state_call(dec, state_ssm[j], xc, xw, yin, eax, proj_t, nw)
            xs = _out_call(y_s.reshape(ts, B_WIDTH), w_out, xs, mods_s, tm_s, 512, ts)
            raw = jnp.concatenate([state_conv[j], proj_s[:, B_WIDTH:B_WIDTH + B_CONV_DIM].reshape(nb, lq, B_CONV_DIM)], axis=1)
            outs["ssm_s"].append(h_last)
            outs["conv_s"].append(raw[:, -(B_CONV - 1):])
        else:
            w_in = c_w_in[j].astype(BF16)
            w_out = c_w_out[j].astype(BF16)
            bias_p = jnp.repeat(c_bs[j].T, C_GDIM, axis=1)
            proj_p = _in_call(xp, norm_w[l], mods_p, w_in, tm_p, 1024, seq)
            (m_p,) = _gmlp_call(proj_p, c_ws[j], bias_p, c_ln_g[j], c_ln_b[j], False)
            xp = _out_call(m_p, w_out, xp, mods_p, tm_p, 512, seq)

            reps = CHUNK // lq
            wm_s = jnp.einsum("ab,gts->gatbs", jnp.eye(reps, dtype=F32), c_ws[j][:, :lq, :lq]).reshape(C_GROUPS, CHUNK, CHUNK)
            bias_s = jnp.tile(jnp.repeat(c_bs[j][:, :lq].T, C_GDIM, axis=1), (reps, 1))
            proj_s = _in_call(xs, norm_w[l], mods_s, w_in, tm_s, 1024, ts)
            m_s, v_s = _gmlp_call(proj_s, wm_s, bias_s, c_ln_g[j], c_ln_b[j], True)
            xs = _out_call(m_s, w_out, xs, mods_s, tm_s, 512, ts)
            outs["v_s"].append(v_s.reshape(nb, lq, C_WIDTH))

    y_p = _norm_call(xp, norm_f, 1024).reshape(bsz, seq, D_MODEL)
    y_s = _norm_call(xs, norm_f, ts).reshape(nb, lq, D_MODEL)
    return (y_p, y_s,
            jnp.stack(outs["ckv_p"]), jnp.stack(outs["kpe_p"]), jnp.stack(outs["ckv_s"]), jnp.stack(outs["kpe_s"]),
            jnp.stack(outs["ssm_p"]), jnp.stack(outs["conv_p"]), jnp.stack(outs["ssm_s"]), jnp.stack(outs["conv_s"]),
            jnp.stack(outs["v_s"]))
```

```python
import functools
import math

import jax
import jax.numpy as jnp
from jax import lax
from jax.experimental import pallas as pl
from jax.experimental.pallas import tpu as pltpu

F32 = jnp.float32
BF16 = jnp.bfloat16

D_MODEL = 2048
DEPTH = 4
EPS = 1e-6
A_HEADS = 16
A_NOPE = 128
A_ROPE = 64
A_VDIM = 128
A_Q_LORA = 512
A_KV_LORA = 512
A_WIDTH = A_HEADS * A_VDIM
A_SCALE = (A_NOPE + A_ROPE) ** -0.5
ROPE_THETA = 10000.0
PAGE_SIZE = 128
B_WIDTH = 2 * D_MODEL
B_HEADDIM = 64
B_HEADS = B_WIDTH // B_HEADDIM
B_GROUPS = 8
B_STATE = 128
B_CONV = 4
B_GN = B_GROUPS * B_STATE
B_CONV_DIM = B_WIDTH + 2 * B_GN
B_HPG = B_HEADS // B_GROUPS
B_GW = B_WIDTH // B_GROUPS
CHUNK = 128
C_WIDTH = 2 * D_MODEL
C_GROUPS = 16
C_GDIM = C_WIDTH // C_GROUPS

LANES = 128
SUBLANES = 8
VMEM_BYTES = 64 * 1024 * 1024

A_IN_PAD = 3200
A_QK = A_KV_LORA + LANES
B_IN_PAD = B_WIDTH + B_CONV_DIM + LANES
NEG = -1e30

NT_DIMS = (((1,), (1,)), ((), ()))
TN_DIMS = (((0,), (0,)), ((), ()))


def _cparams(sem, vmem_mb):
    return pltpu.CompilerParams(dimension_semantics=sem, vmem_limit_bytes=vmem_mb * 1024 * 1024)


def _silu(x):
    return x * (1.0 / (1.0 + jnp.exp(-x)))


def _gelu(x):
    return 0.5 * x * (1.0 + jnp.tanh(math.sqrt(2.0 / math.pi) * (x + 0.044715 * (x * x * x))))


def _softplus(x):
    return jnp.maximum(x, 0.0) + jnp.log1p(jnp.exp(-jnp.abs(x)))


def _rms(x, w):
    return x * lax.rsqrt(jnp.mean(x * x, axis=-1, keepdims=True) + EPS) * w


def _dot(a, b):
    return jnp.dot(a, b, preferred_element_type=F32)


def _dot_nt(a, b):
    return lax.dot_general(a, b, NT_DIMS, preferred_element_type=F32)


def _dot_tn(a, b):
    return lax.dot_general(a, b, TN_DIMS, preferred_element_type=F32)


def _split_dot(a, sel, parts, left=False):
    acc = None
    rem = a
    for _ in range(parts):
        piece = rem.astype(BF16)
        rem = rem - piece.astype(F32)
        term = _dot(sel, piece) if left else _dot(piece, sel)
        acc = term if acc is None else acc + term
    return acc


def _ada_kernel(c_ref, w_ref, b_ref, o_ref):
    s = _silu(c_ref[...]).astype(BF16)
    o_ref[0] = _dot(s, w_ref[0].astype(BF16)) + b_ref[0]


def _ada_call(c_all, ada_w, ada_b):
    n_rows = c_all.shape[0]
    tn = 1024
    return pl.pallas_call(
        _ada_kernel,
        grid=(DEPTH, 3 * D_MODEL // tn),
        in_specs=[
            pl.BlockSpec((n_rows, D_MODEL), lambda l, j: (0, 0)),
            pl.BlockSpec((1, D_MODEL, tn), lambda l, j: (l, 0, j)),
            pl.BlockSpec((1, 1, tn), lambda l, j: (l, 0, j)),
        ],
        out_specs=pl.BlockSpec((1, n_rows, tn), lambda l, j: (l, 0, j)),
        out_shape=jax.ShapeDtypeStruct((DEPTH, n_rows, 3 * D_MODEL), F32),
        compiler_params=_cparams(("parallel", "parallel"), 40),
        name="ada_mod",
    )(c_all, ada_w, ada_b.reshape(DEPTH, 1, 3 * D_MODEL))


def _in_kernel(x_ref, nw_ref, sh_ref, sc_ref, w_ref, o_ref, h_ref):
    @pl.when(pl.program_id(1) == 0)
    def _():
        y = _rms(x_ref[...], nw_ref[...])
        h_ref[...] = (y * (1.0 + sc_ref[0]) + sh_ref[0]).astype(BF16)

    o_ref[...] = _dot(h_ref[...], w_ref[...])


def _in_call(x, nw, mods, w, tm, tn, rows_per_mod):
    t, n = x.shape[0], w.shape[1]
    r = mods.shape[1]
    return pl.pallas_call(
        _in_kernel,
        grid=(t // tm, n // tn),
        in_specs=[
            pl.BlockSpec((tm, D_MODEL), lambda i, j: (i, 0)),
            pl.BlockSpec((1, D_MODEL), lambda i, j: (0, 0)),
            pl.BlockSpec((1, r, D_MODEL), lambda i, j: (i * tm // rows_per_mod, 0, 0)),
            pl.BlockSpec((1, r, D_MODEL), lambda i, j: (i * tm // rows_per_mod, 0, 1)),
            pl.BlockSpec((D_MODEL, tn), lambda i, j: (0, j)),
        ],
        out_specs=pl.BlockSpec((tm, tn), lambda i, j: (i, j)),
        out_shape=jax.ShapeDtypeStruct((t, n), F32),
        scratch_shapes=[pltpu.VMEM((tm, D_MODEL), BF16)],
        compiler_params=_cparams(("parallel", "arbitrary"), 56),
        name="in_proj",
    )(x, nw.reshape(1, D_MODEL), mods, mods, w)


def _out_kernel(a_ref, w_ref, x_ref, g_ref, o_ref):
    o_ref[...] = x_ref[...] + g_ref[0] * _dot(a_ref[...].astype(BF16), w_ref[...])


def _out_call(a, w, x, mods, tm, tn, rows_per_mod):
    t, k = a.shape
    r = mods.shape[1]
    nj = D_MODEL // tn
    return pl.pallas_call(
        _out_kernel,
        grid=(t // tm, nj),
        in_specs=[
            pl.BlockSpec((tm, k), lambda i, j: (i, 0)),
            pl.BlockSpec((k, tn), lambda i, j: (0, j)),
            pl.BlockSpec((tm, tn), lambda i, j: (i, j)),
            pl.BlockSpec((1, r, tn), lambda i, j: (i * tm // rows_per_mod, 0, 2 * nj + j)),
        ],
        out_specs=pl.BlockSpec((tm, tn), lambda i, j: (i, j)),
        out_shape=jax.ShapeDtypeStruct((t, D_MODEL), F32),
        compiler_params=_cparams(("parallel", "parallel"), 56),
        name="out_proj",
    )(a, w, x, mods)


def _norm_kernel(x_ref, w_ref, o_ref):
    o_ref[...] = _rms(x_ref[...], w_ref[...])


def _norm_call(x, w, tm):
    t = x.shape[0]
    return pl.pallas_call(
        _norm_kernel,
        grid=(t // tm,),
        in_specs=[pl.BlockSpec((tm, D_MODEL), lambda i: (i, 0)), pl.BlockSpec((1, D_MODEL), lambda i: (0, 0))],
        out_specs=pl.BlockSpec((tm, D_MODEL), lambda i: (i, 0)),
        out_shape=jax.ShapeDtypeStruct((t, D_MODEL), F32),
        compiler_params=_cparams(("parallel",), 40),
        name="final_norm",
    )(x, w.reshape(1, D_MODEL))


def _rope(x, cos, ssin):
    lane = lax.broadcasted_iota(jnp.int32, x.shape, 1)
    first = (lane % A_ROPE) < (A_ROPE // 2)
    partner = jnp.where(first, pltpu.roll(x, LANES - A_ROPE // 2, 1), pltpu.roll(x, A_ROPE // 2, 1))
    return x * cos + partner * ssin


def _qkv_kernel(cq_ref, ckv_ref, kpe_ref, cos_ref, sin_ref, qn_ref, kvn_ref, wuq_ref, wk_ref,
                qcat_ref, kcat_ref, ckv_o, kpe_o):
    cos, ssin = cos_ref[...], sin_ref[...]
    ckv = _rms(ckv_ref[...], kvn_ref[...])
    ckv_o[...] = ckv
    kpe = _rope(kpe_ref[...], cos, ssin)
    kpe_o[...] = kpe
    kcat_ref[:, 0:A_KV_LORA] = ckv.astype(BF16)
    kcat_ref[:, A_KV_LORA:A_QK] = kpe.astype(BF16)

    q = _dot(_rms(cq_ref[...], qn_ref[...]).astype(BF16), wuq_ref[...])
    for h in range(A_HEADS):
        qh = q[:, h * A_NOPE:(h + 1) * A_NOPE].astype(BF16)
        qcat_ref[h, :, 0:A_KV_LORA] = (_dot(qh, wk_ref[h]) * A_SCALE).astype(qcat_ref.dtype)
    pe0 = A_HEADS * A_NOPE
    for pr in range(A_HEADS // 2):
        qp = _rope(q[:, pe0 + pr * LANES:pe0 + (pr + 1) * LANES], cos, ssin) * A_SCALE
        qcat_ref[2 * pr, :, A_KV_LORA:A_QK] = qp.astype(qcat_ref.dtype)
        qcat_ref[2 * pr + 1, :, A_KV_LORA:A_QK] = pltpu.roll(qp, A_ROPE, 1).astype(qcat_ref.dtype)


def _qkv_call(proj, cos, ssin, qn, kvn, wuq, wk, tm, q_dtype=BF16):
    t = proj.shape[0]
    cq_blk = D_MODEL // A_Q_LORA
    return pl.pallas_call(
        _qkv_kernel,
        grid=(t // tm,),
        in_specs=[
            pl.BlockSpec((tm, A_Q_LORA), lambda i: (i, cq_blk)),
            pl.BlockSpec((tm, A_KV_LORA), lambda i: (i, cq_blk + 1)),
            pl.BlockSpec((tm, LANES), lambda i: (i, (D_MODEL + A_Q_LORA + A_KV_LORA) // LANES)),
            pl.BlockSpec((tm, LANES), lambda i: (i, 0)),
            pl.BlockSpec((tm, LANES), lambda i: (i, 0)),
            pl.BlockSpec((1, A_Q_LORA), lambda i: (0, 0)),
            pl.BlockSpec((1, A_KV_LORA), lambda i: (0, 0)),
            pl.BlockSpec(wuq.shape, lambda i: (0, 0)),
            pl.BlockSpec(wk.shape, lambda i: (0, 0, 0)),
        ],
        out_specs=[
            pl.BlockSpec((A_HEADS, tm, A_QK), lambda i: (0, i, 0)),
            pl.BlockSpec((tm, A_QK), lambda i: (i, 0)),
            pl.BlockSpec((tm, A_KV_LORA), lambda i: (i, 0)),
            pl.BlockSpec((tm, LANES), lambda i: (i, 0)),
        ],
        out_shape=[
            jax.ShapeDtypeStruct((A_HEADS, t, A_QK), q_dtype),
            jax.ShapeDtypeStruct((t, A_QK), BF16),
            jax.ShapeDtypeStruct((t, A_KV_LORA), F32),
            jax.ShapeDtypeStruct((t, LANES), F32),
        ],
        compiler_params=_cparams(("parallel",), 48),
        name="mla_qkv",
    )(proj, proj, proj, cos, ssin, qn.reshape(1, -1), kvn.reshape(1, -1), wuq, wk)


def _attn_prompt_kernel(q_ref, k_ref, gate_ref, wv_ref, o_ref, *scratch, tq, tk, nk, hg):
    i, j = pl.program_id(1), pl.program_id(2)
    groups = A_HEADS // hg
    m_refs, l_refs, acc_refs = scratch[:groups], scratch[groups:2 * groups], scratch[2 * groups:]

    @pl.when(j == 0)
    def _():
        for g in range(groups):
            m_refs[g][...] = jnp.full(m_refs[g].shape, NEG, F32)
            l_refs[g][...] = jnp.zeros(l_refs[g].shape, F32)
            acc_refs[g][...] = jnp.zeros(acc_refs[g].shape, F32)

    def step(masked, width):
        k = k_ref[0:width, :]
        v = k[:, 0:A_KV_LORA]

        def scores(g):
            return _dot_nt(q_ref[g * hg:(g + 1) * hg].reshape(hg * tq, A_QK), k)

        s_next = scores(0)
        for g in range(groups):
            s = s_next
            if g + 1 < groups:
                s_next = scores(g + 1)
            if masked:
                qpos = i * tq + lax.broadcasted_iota(jnp.int32, (hg, tq, width), 1).reshape(hg * tq, width)
                kpos = j * tk + lax.broadcasted_iota(jnp.int32, (hg * tq, width), 1)
                s = jnp.where(kpos <= qpos, s, NEG)
            m_old = m_refs[g][...]
            m_new = jnp.maximum(m_old, jnp.max(s, axis=-1, keepdims=True))
            p = jnp.exp(s - m_new)
            alpha = jnp.exp(m_old - m_new)
            l_refs[g][...] = alpha * l_refs[g][...] + jnp.sum(p, axis=-1, keepdims=True)
            acc_refs[g][...] = alpha * acc_refs[g][...] + _dot(p.astype(BF16), v)
            m_refs[g][...] = m_new

    first_q, last_q = i * tq, i * tq + tq - 1
    pl.when(j * tk + tk - 1 <= first_q)(lambda: step(False, tk))
    pl.when((j * tk <= last_q) & (j * tk + tk - 1 > first_q))(lambda: step(True, tk))

    @pl.when(j == nk - 1)
    def _():
        for h in range(A_HEADS):
            g, sl = h // hg, slice((h % hg) * tq, (h % hg + 1) * tq)
            o_lat = (acc_refs[g][sl, :] / l_refs[g][sl, :]).astype(BF16)
            cols = slice(h * A_VDIM, (h + 1) * A_VDIM)
            o_ref[:, cols] = (_dot(o_lat, wv_ref[h]) * _silu(gate_ref[:, cols])).astype(BF16)


def _attn_prompt_call(qcat, kcat, proj, wv, bsz, seq, tq, tk, hg=A_HEADS):
    nq, nk = seq // tq, seq // tk
    kern = functools.partial(_attn_prompt_kernel, tq=tq, tk=tk, nk=nk, hg=hg)
    groups, rows = A_HEADS // hg, hg * tq
    scratch = ([pltpu.VMEM((rows, 1), F32)] * (2 * groups)) + ([pltpu.VMEM((rows, A_KV_LORA), F32)] * groups)
    return pl.pallas_call(
        kern,
        grid=(bsz, nq, nk),
        in_specs=[
            pl.BlockSpec((A_HEADS, tq, A_QK), lambda b, i, j: (0, b * nq + i, 0)),
            pl.BlockSpec((tk, A_QK), lambda b, i, j: (b * nk + jnp.minimum(j, (i * tq + tq - 1) // tk), 0)),
            pl.BlockSpec((tq, A_WIDTH), lambda b, i, j: (b * nq + i, 0)),
            pl.BlockSpec(wv.shape, lambda b, i, j: (0, 0, 0)),
        ],
        out_specs=pl.BlockSpec((tq, A_WIDTH), lambda b, i, j: (b * nq + i, 0)),
        out_shape=jax.ShapeDtypeStruct((bsz * seq, A_WIDTH), BF16),
        scratch_shapes=scratch,
        compiler_params=_cparams(("parallel", "parallel", "arbitrary"), 56),
        name="mla_attn_prompt",
    )(qcat, kcat, proj, wv)


def _attn_sample_kernel(pt_ref, q_ref, ckvn_ref, kpen_ref, ckv_hbm, kpe_hbm, o_ref, m_ref, l_ref, acc_ref,
                        kbuf, pbuf, qbuf, knbuf, kraw, praw, sem, *, layer, pages, nsteps, lq, chains):
    b, c = pl.program_id(0), pl.program_id(1)
    rows = qbuf.shape[0]
    toks = [pl.ds((b % SUBLANES) * lq + t, 1) for t in range(lq)]
    step = b * nsteps + c
    slot = step % 2

    def page_copies(seq, chunk, sl):
        cps = []
        for n in range(pages):
            page = 0 if seq is None else pt_ref[seq, chunk * pages + n]
            cps.append(pltpu.make_async_copy(ckv_hbm.at[layer, page], kraw.at[sl, n], sem.at[0, sl]))
            cps.append(pltpu.make_async_copy(kpe_hbm.at[layer, page], praw.at[sl, n], sem.at[1, sl]))
        return cps

    @pl.when(step == 0)
    def _():
        for cp in page_copies(0, 0, 0):
            cp.start()

    @pl.when(step + 1 < pl.num_programs(0) * nsteps)
    def _():
        for cp in page_copies((step + 1) // nsteps, (step + 1) % nsteps, 1 - slot):
            cp.start()

    for cp in page_copies(None, None, slot):
        cp.wait()

    @pl.when(c == 0)
    def _():
        m_ref[...] = jnp.full(m_ref.shape, NEG, F32)
        l_ref[...] = jnp.zeros(l_ref.shape, F32)
        acc_ref[...] = jnp.zeros(acc_ref.shape, F32)
        qbuf[...] = jnp.zeros(qbuf.shape, F32)
        knbuf[...] = jnp.zeros(knbuf.shape, F32)
        for t, tok in enumerate(toks):
            for h in range(A_HEADS):
                qbuf[h * lq + t:h * lq + t + 1, :] = q_ref[h, tok, :]
            knbuf[t:t + 1, 0:A_KV_LORA] = ckvn_ref[tok, :]
            knbuf[t:t + 1, A_KV_LORA:A_QK] = kpen_ref[tok, :]

    q = qbuf[...].astype(BF16)
    ql, qp = q[:, 0:A_KV_LORA], q[:, A_KV_LORA:A_KV_LORA + A_ROPE]

    def as_col(row):
        return jnp.broadcast_to(row, (rows, rows)).T[:, 0:1]

    def local(st, vals):
        m = jnp.max(st, axis=0, keepdims=True)
        pt = jnp.exp(st - m)
        return m, jnp.sum(pt, axis=0, keepdims=True), _dot_tn(pt.astype(BF16), vals)

    def merge(parts):
        m_old = m_ref[...]
        m_new = m_old
        for m, _, _ in parts:
            m_new = jnp.maximum(m_new, m)
        alpha = jnp.exp(m_old - m_new)
        l_new = alpha * l_ref[...]
        acc = as_col(alpha) * acc_ref[...]
        for m, l, o in parts:
            w = jnp.exp(m - m_new)
            l_new = l_new + w * l
            acc = acc + as_col(w) * o
        m_ref[...], l_ref[...], acc_ref[...] = m_new, l_new, acc

    per = pages // chains

    def scores(ch):
        for n in range(ch * per, (ch + 1) * per):
            kbuf[n * PAGE_SIZE:(n + 1) * PAGE_SIZE, :] = kraw[slot, n].astype(BF16)
            pbuf[n * PAGE_SIZE:(n + 1) * PAGE_SIZE, :] = praw[slot, n].T.astype(BF16)
        sl = slice(ch * per * PAGE_SIZE, (ch + 1) * per * PAGE_SIZE)
        keys = kbuf[sl, :]
        return _dot_nt(keys, ql) + _dot_nt(pbuf[sl, :], qp), keys

    parts = []
    nxt = scores(0)
    for ch in range(chains):
        st, keys = nxt
        if ch + 1 < chains:
            nxt = scores(ch + 1)
        parts.append(local(st, keys))
    merge(parts)

    @pl.when(c == nsteps - 1)
    def _():
        kn = knbuf[...].astype(BF16)
        t_k = lax.broadcasted_iota(jnp.int32, (PAGE_SIZE, rows), 0)
        t_q = lax.broadcasted_iota(jnp.int32, (PAGE_SIZE, rows), 1) % lq
        merge([local(jnp.where(t_k <= t_q, _dot_nt(kn, q), NEG), kn[:, 0:A_KV_LORA])])
        o = acc_ref[...] / as_col(l_ref[...])
        for h in range(A_HEADS):
            for t, tok in enumerate(toks):
                o_ref[h, tok, :] = o[h * lq + t:h * lq + t + 1, :]


def _attn_sample_call(page_table, qcat, ckv_new, kpe_new, cache_ckv, cache_kpe_t, layer, lq, pages, chains=2):
    nb, n_pages = page_table.shape
    nsteps = n_pages // pages
    rows = -(-A_HEADS * lq // LANES) * LANES
    blk = SUBLANES * lq
    kern = functools.partial(_attn_sample_kernel, layer=layer, pages=pages, nsteps=nsteps, lq=lq, chains=chains)
    in_specs = [
        pl.BlockSpec((A_HEADS, blk, A_QK), lambda b, c, pt: (0, b // SUBLANES, 0)),
        pl.BlockSpec((blk, A_KV_LORA), lambda b, c, pt: (b // SUBLANES, 0)),
        pl.BlockSpec((blk, LANES), lambda b, c, pt: (b // SUBLANES, 0)),
        pl.BlockSpec(memory_space=pl.ANY),
        pl.BlockSpec(memory_space=pl.ANY),
    ]
    grid_spec = pltpu.PrefetchScalarGridSpec(
        num_scalar_prefetch=1,
        grid=(nb, nsteps),
        in_specs=in_specs,
        out_specs=pl.BlockSpec((A_HEADS, blk, A_KV_LORA), lambda b, c, pt: (0, b // SUBLANES, 0)),
        scratch_shapes=[pltpu.VMEM((1, rows), F32), pltpu.VMEM((1, rows), F32), pltpu.VMEM((rows, A_KV_LORA), F32),
                        pltpu.VMEM((pages * PAGE_SIZE, A_KV_LORA), BF16), pltpu.VMEM((pages * PAGE_SIZE, A_ROPE), BF16),
                        pltpu.VMEM((rows, A_QK), F32), pltpu.VMEM((PAGE_SIZE, A_QK), F32),
                        pltpu.VMEM((2, pages, PAGE_SIZE, A_KV_LORA), F32), pltpu.VMEM((2, pages, A_ROPE, PAGE_SIZE), F32),
                        pltpu.SemaphoreType.DMA((2, 2))],
    )
    return pl.pallas_call(
        kern,
        grid_spec=grid_spec,
        out_shape=jax.ShapeDtypeStruct((A_HEADS, nb * lq, A_KV_LORA), F32),
        compiler_params=_cparams(("arbitrary", "arbitrary"), 48),
        name="mla_attn_sample",
    )(page_table, qcat, ckv_new, kpe_new, cache_ckv, cache_kpe_t)


def _oproj_kernel(o_ref, wv_ref, gate_ref, out_ref):
    out_ref[...] = (_dot(o_ref[0].astype(BF16), wv_ref[0]) * _silu(gate_ref[...])).astype(BF16)


def _oproj_call(o_lat, wv, proj):
    t = o_lat.shape[1]
    return pl.pallas_call(
        _oproj_kernel,
        grid=(A_HEADS,),
        in_specs=[
            pl.BlockSpec((1, t, A_KV_LORA), lambda h: (h, 0, 0)),
            pl.BlockSpec((1, A_KV_LORA, A_VDIM), lambda h: (h, 0, 0)),
            pl.BlockSpec((t, A_VDIM), lambda h: (0, h)),
        ],
        out_specs=pl.BlockSpec((t, A_VDIM), lambda h: (0, h)),
        out_shape=jax.ShapeDtypeStruct((t, A_WIDTH), BF16),
        compiler_params=_cparams(("parallel",), 32),
        name="mla_oproj_sample",
    )(o_lat, wv, proj)


def _group_norm(y, nw):
    outs = []
    for g in range(B_GROUPS):
        sl = slice(g * B_GW, (g + 1) * B_GW)
        outs.append(_rms(y[:, sl], nw[:, sl]))
    return jnp.concatenate(outs, axis=1)


def _ssd_prompt_kernel(proj_ref, cw_ref, cb_ref, dtb_ref, alog_ref, dexp_ref, nw_ref, e_ref, tril_ref,
                       y_ref, tail_ref, hlast_ref, xp_ref, h_ref, yacc_ref, *, nchunks):
    c = pl.program_id(1)
    q = CHUNK

    @pl.when(c == 0)
    def _():
        xp_ref[0:SUBLANES, :] = jnp.zeros((SUBLANES, B_CONV_DIM), F32)
        h_ref[...] = jnp.zeros(h_ref.shape, F32)

    raw = proj_ref[:, B_WIDTH:B_WIDTH + B_CONV_DIM]
    xp_ref[SUBLANES:SUBLANES + q, :] = raw
    conv = cb_ref[...] + cw_ref[3:4, :] * raw
    for back in range(1, B_CONV):
        conv = conv + cw_ref[3 - back:4 - back, :] * xp_ref[SUBLANES - back:SUBLANES - back + q, :]
    xp_ref[0:SUBLANES, :] = raw[q - SUBLANES:q, :]
    tail_ref[0] = raw[q - SUBLANES:q, :]
    xc = _silu(conv)
    x = xc[:, 0:B_WIDTH]
    bm = xc[:, B_WIDTH:B_WIDTH + B_GN].astype(BF16)
    cm = xc[:, B_WIDTH + B_GN:B_CONV_DIM].astype(BF16)

    dt = _softplus(proj_ref[:, B_WIDTH + B_CONV_DIM:B_IN_PAD] + dtb_ref[...])
    dta = dt * (-jnp.exp(alog_ref[...]))
    acum = _split_dot(dta, tril_ref[...], 3, left=True)
    acum_t = acum.T
    alast = acum[q - 1:q, :]
    e = e_ref[...]
    dtx = _split_dot(dt, e, 2)
    eax = _split_dot(jnp.exp(acum), e, 2)
    wx = _split_dot(jnp.exp(alast - acum) * dt, e, 2)
    lane = lax.broadcasted_iota(jnp.int32, (q, B_WIDTH), 1)
    even = (lane % LANES) < B_HEADDIM
    xdt = x * dtx
    xdt_even = jnp.where(even, xdt, 0.0).astype(BF16)
    xdt_odd = jnp.where(even, 0.0, xdt).astype(BF16)
    xw = (x * wx).astype(BF16)
    causal = lax.broadcasted_iota(jnp.int32, (q, q), 0) >= lax.broadcasted_iota(jnp.int32, (q, q), 1)
    dec_col = jnp.exp(acum_t[:, q - 1:q])

    for g in range(B_GROUPS):
        cg = cm[:, g * B_STATE:(g + 1) * B_STATE]
        bg = bm[:, g * B_STATE:(g + 1) * B_STATE]
        cb = _dot_nt(cg, bg)
        hg = h_ref[g * B_HPG:(g + 1) * B_HPG].reshape(B_GW, B_STATE)
        ystate = _dot_nt(cg, hg.astype(BF16))
        for pr in range(B_HPG // 2):
            ls = []
            for hh in (2 * pr, 2 * pr + 1):
                h = g * B_HPG + hh
                diff = acum[:, h:h + 1] - acum_t[h:h + 1, :]
                ls.append((cb * jnp.exp(jnp.where(causal, diff, NEG))).astype(BF16))
            col = g * B_GW + pr * LANES
            lhs = jnp.concatenate(ls, axis=1)
            rhs = jnp.concatenate([xdt_even[:, col:col + LANES], xdt_odd[:, col:col + LANES]], axis=0)
            yacc_ref[:, col:col + LANES] = _dot(lhs, rhs) + ystate[:, pr * LANES:(pr + 1) * LANES] * eax[:, col:col + LANES]
        snew = _dot_tn(xw[:, g * B_GW:(g + 1) * B_GW], bg)
        for hh in range(B_HPG):
            h = g * B_HPG + hh
            dec = jnp.broadcast_to(dec_col[h:h + 1, :], (B_HEADDIM, B_STATE))
            h_ref[h] = h_ref[h] * dec + snew[hh * B_HEADDIM:(hh + 1) * B_HEADDIM, :]

    y = (yacc_ref[...] + dexp_ref[...] * x) * _silu(proj_ref[:, 0:B_WIDTH])
    y_ref[...] = _group_norm(y, nw_ref[...]).astype(BF16)

    @pl.when(c == nchunks - 1)
    def _():
        hlast_ref[0] = h_ref[...]


def _ssd_prompt_call(proj, cw, cb, dtb, alog, dexp, nw, e, tril, bsz, seq):
    nchunks = seq // CHUNK
    kern = functools.partial(_ssd_prompt_kernel, nchunks=nchunks)
    const = lambda b, c: (0, 0)
    return pl.pallas_call(
        kern,
        grid=(bsz, nchunks),
        in_specs=[
            pl.BlockSpec((CHUNK, B_IN_PAD), lambda b, c: (b * nchunks + c, 0)),
            pl.BlockSpec((B_CONV, B_CONV_DIM), const),
            pl.BlockSpec((1, B_CONV_DIM), const),
            pl.BlockSpec((1, LANES), const),
            pl.BlockSpec((1, LANES), const),
            pl.BlockSpec((1, B_WIDTH), const),
            pl.BlockSpec((1, B_WIDTH), const),
            pl.BlockSpec((LANES, B_WIDTH), const),
            pl.BlockSpec((CHUNK, CHUNK), const),
        ],
        out_specs=[
            pl.BlockSpec((CHUNK, B_WIDTH), lambda b, c: (b * nchunks + c, 0)),
            pl.BlockSpec((1, SUBLANES, B_CONV_DIM), lambda b, c: (b, 0, 0)),
            pl.BlockSpec((1, B_HEADS, B_HEADDIM, B_STATE), lambda b, c: (b, 0, 0, 0)),
        ],
        out_shape=[
            jax.ShapeDtypeStruct((bsz * seq, B_WIDTH), BF16),
            jax.ShapeDtypeStruct((bsz, SUBLANES, B_CONV_DIM), F32),
            jax.ShapeDtypeStruct((bsz, B_HEADS, B_HEADDIM, B_STATE), F32),
        ],
        scratch_shapes=[
            pltpu.VMEM((SUBLANES + CHUNK, B_CONV_DIM), F32),
            pltpu.VMEM((B_HEADS, B_HEADDIM, B_STATE), F32),
            pltpu.VMEM((CHUNK, B_WIDTH), F32),
        ],
        compiler_params=_cparams(("parallel", "arbitrary"), 56),
        name="ssd_prompt",
    )(proj, cw, cb, dtb, alog, dexp, nw, e, tril)


def _ssd_sample_conv_kernel(x_ref, prev_ref, cw_ref, cb_ref, o_ref, *, lq):
    rows = [prev_ref[n] for n in range(B_CONV - 1)] + [x_ref[n] for n in range(lq)]
    for t in range(lq):
        conv = cb_ref[...]
        for j in range(B_CONV):
            conv = conv + cw_ref[j:j + 1, :] * rows[t + j]
        o_ref[t] = _silu(conv)


def _ssd_sample_conv_call(proj_t, prev_t, cw, cb):
    lq, nb = proj_t.shape[:2]
    tc = 2048
    off = B_WIDTH // tc
    kern = functools.partial(_ssd_sample_conv_kernel, lq=lq)
    return pl.pallas_call(
        kern,
        grid=(B_CONV_DIM // tc,),
        in_specs=[
            pl.BlockSpec((lq, nb, tc), lambda j: (0, 0, off + j)),
            pl.BlockSpec((B_CONV - 1, nb, tc), lambda j: (0, 0, j)),
            pl.BlockSpec((B_CONV, tc), lambda j: (0, j)),
            pl.BlockSpec((1, tc), lambda j: (0, j)),
        ],
        out_specs=pl.BlockSpec((lq, nb, tc), lambda j: (0, 0, j)),
        out_shape=jax.ShapeDtypeStruct((lq, nb, B_CONV_DIM), F32),
        compiler_params=_cparams(("parallel",), 40),
        name="ssd_sample_conv",
    )(proj_t, prev_t, cw, cb)


def _ssd_sample_intra_kernel(xc_ref, dt_ref, dtb_ref, alog_ref, dexp_ref, r_ref, e_ref,
                             yin_ref, eax_ref, xw_ref, dec_ref, *, lq):
    qi = pl.program_id(0)
    a = -jnp.exp(alog_ref[...])
    dts = [_softplus(dt_ref[k] + dtb_ref[...]) for k in range(lq)]
    acums = []
    run = None
    for k in range(lq):
        run = dts[k] * a if run is None else run + dts[k] * a
        acums.append(run)
    acum_q = acums[0]
    for k in range(1, lq):
        acum_q = jnp.where(qi >= k, acums[k], acum_q)
    e, r = e_ref[...], r_ref[...]
    x_q = xc_ref[qi, :, 0:B_WIDTH]
    c_q = xc_ref[qi, :, B_WIDTH + B_GN:B_CONV_DIM]
    y = dexp_ref[...] * x_q
    for k in range(lq):
        b_k = xc_ref[k, :, B_WIDTH:B_WIDTH + B_GN]
        cb = _split_dot(c_q * b_k, r, 2)
        m = cb * jnp.exp(jnp.where(qi >= k, acum_q - acums[k], NEG)) * dts[k]
        y = y + _split_dot(m, e, 2) * xc_ref[k, :, 0:B_WIDTH]
    yin_ref[0] = y
    eax_ref[0] = _split_dot(jnp.exp(acum_q), e, 2)
    dt_q = dts[0]
    for k in range(1, lq):
        dt_q = jnp.where(qi == k, dts[k], dt_q)
    xw_ref[0] = x_q * _split_dot(jnp.exp(acums[lq - 1] - acum_q) * dt_q, e, 2)
    dec_ref[...] = jnp.exp(acums[lq - 1])


def _ssd_sample_intra_call(xc, proj_t, dtb, alog, dexp, r, e):
    lq, nb = xc.shape[:2]
    kern = functools.partial(_ssd_sample_intra_kernel, lq=lq)
    tok = lambda q: (q, 0, 0)
    const = lambda q: (0, 0)
    big = jax.ShapeDtypeStruct((lq, nb, B_WIDTH), F32)
    return pl.pallas_call(
        kern,
        grid=(lq,),
        in_specs=[
            pl.BlockSpec((lq, nb, B_CONV_DIM), lambda q: (0, 0, 0)),
            pl.BlockSpec((lq, nb, LANES), lambda q: (0, 0, (B_WIDTH + B_CONV_DIM) // LANES)),
            pl.BlockSpec((1, LANES), const),
            pl.BlockSpec((1, LANES), const),
            pl.BlockSpec((1, B_WIDTH), const),
            pl.BlockSpec((B_GN, LANES), const),
            pl.BlockSpec((LANES, B_WIDTH), const),
        ],
        out_specs=[
            pl.BlockSpec((1, nb, B_WIDTH), tok),
            pl.BlockSpec((1, nb, B_WIDTH), tok),
            pl.BlockSpec((1, nb, B_WIDTH), tok),
            pl.BlockSpec((nb, LANES), const),
        ],
        out_shape=[big, big, big, jax.ShapeDtypeStruct((nb, LANES), F32)],
        compiler_params=_cparams(("arbitrary",), 56),
        name="ssd_sample_intra",
    )(xc, proj_t, dtb, alog, dexp, r, e)


def _ssd_sample_state_kernel(dec_ref, h0_ref, c_ref, b_ref, xw_ref, yin_ref, eax_ref, z_ref, nw_ref,
                             y_ref, hl_ref, c8, b8, xw8, *, lq):
    s = pl.program_id(0)
    r = pl.ds(s % SUBLANES, 1)
    for buf in (c8, b8, xw8):
        buf[lq:SUBLANES, :] = jnp.zeros((SUBLANES - lq, buf.shape[1]), F32)
    for t in range(lq):
        c8[t:t + 1, :] = c_ref[t, r, :]
        b8[t:t + 1, :] = b_ref[t, r, :]
        xw8[t:t + 1, :] = xw_ref[t, r, :]
    ys = []
    for g in range(B_GROUPS):
        hg = h0_ref[0, g * B_HPG:(g + 1) * B_HPG].reshape(B_GW, B_STATE).astype(BF16)
        cg = c8[:, g * B_STATE:(g + 1) * B_STATE].astype(BF16)
        bg = b8[:, g * B_STATE:(g + 1) * B_STATE].astype(BF16)
        ys.append(_dot_nt(cg, hg))
        snew = _dot_tn(xw8[:, g * B_GW:(g + 1) * B_GW].astype(BF16), bg)
        for hh in range(B_HPG):
            h = g * B_HPG + hh
            hl_ref[0, h] = h0_ref[0, h] * dec_ref[s, h] + snew[hh * B_HEADDIM:(hh + 1) * B_HEADDIM, :]
    ystate = jnp.concatenate(ys, axis=1)
    for t in range(lq):
        y = (yin_ref[t, r, :] + ystate[t:t + 1, :] * eax_ref[t, r, :]) * _silu(z_ref[t, r, :])
        y_ref[0, t:t + 1, :] = _group_norm(y, nw_ref[...])


def _ssd_sample_state_call(dec, h0, xc, xw, yin, eax, proj_t, nw):
    lq, nb = xc.shape[:2]
    kern = functools.partial(_ssd_sample_state_kernel, lq=lq)
    row = lambda blk: (lambda s, d: (0, s // SUBLANES, blk))
    grid_spec = pltpu.PrefetchScalarGridSpec(
        num_scalar_prefetch=1,
        grid=(nb,),
        in_specs=[
            pl.BlockSpec((1, B_HEADS, B_HEADDIM, B_STATE), lambda s, d: (s, 0, 0, 0)),
            pl.BlockSpec((lq, SUBLANES, B_GN), row((B_WIDTH + B_GN) // B_GN)),
            pl.BlockSpec((lq, SUBLANES, B_GN), row(B_WIDTH // B_GN)),
            pl.BlockSpec((lq, SUBLANES, B_WIDTH), row(0)),
            pl.BlockSpec((lq, SUBLANES, B_WIDTH), row(0)),
            pl.BlockSpec((lq, SUBLANES, B_WIDTH), row(0)),
            pl.BlockSpec((lq, SUBLANES, B_WIDTH), row(0)),
            pl.BlockSpec((1, B_WIDTH), lambda s, d: (0, 0)),
        ],
        out_specs=[
            pl.BlockSpec((1, lq, B_WIDTH), lambda s, d: (s, 0, 0)),
            pl.BlockSpec((1, B_HEADS, B_HEADDIM, B_STATE), lambda s, d: (s, 0, 0, 0)),
        ],
        scratch_shapes=[pltpu.VMEM((SUBLANES, B_GN), F32), pltpu.VMEM((SUBLANES, B_GN), F32),
                        pltpu.VMEM((SUBLANES, B_WIDTH), F32)],
    )
    return pl.pallas_call(
        kern,
        grid_spec=grid_spec,
        out_shape=[jax.ShapeDtypeStruct((nb, lq, B_WIDTH), F32),
                   jax.ShapeDtypeStruct((nb, B_HEADS, B_HEADDIM, B_STATE), F32)],
        compiler_params=_cparams(("arbitrary",), 40),
        name="ssd_sample_state",
    )(dec, h0, xc, xc, xw, yin, eax, proj_t, nw)


def _gmlp_kernel(p_ref, wm_ref, bias_ref, lng_ref, lnb_ref, o_ref, *v_refs):
    q = CHUNK
    u = _gelu(p_ref[:, 0:C_WIDTH])
    vp = _gelu(p_ref[:, C_WIDTH:2 * C_WIDTH])
    mu = jnp.mean(vp, axis=-1, keepdims=True)
    var = jnp.mean(jnp.square(vp - mu), axis=-1, keepdims=True)
    v = (vp - mu) * lax.rsqrt(var + EPS) * lng_ref[...] + lnb_ref[...]
    if v_refs:
        v_refs[0][...] = v
    vb = v.astype(BF16)
    causal = lax.broadcasted_iota(jnp.int32, (q, q), 0) >= lax.broadcasted_iota(jnp.int32, (q, q), 1)
    for g in range(C_GROUPS):
        sl = slice(g * C_GDIM, (g + 1) * C_GDIM)
        w = jnp.where(causal, wm_ref[g], 0.0).astype(BF16)
        mix = _dot(w, vb[:, sl]) + bias_ref[:, sl]
        gate = p_ref[:, 2 * C_WIDTH + g * C_GDIM:2 * C_WIDTH + (g + 1) * C_GDIM]
        o_ref[:, sl] = (u[:, sl] * mix * _silu(gate)).astype(BF16)


def _gmlp_call(proj, wm, bias, lng, lnb, want_v):
    t = proj.shape[0]
    row = lambda i: (i, 0)
    const = lambda i: (0, 0)
    out_specs = [pl.BlockSpec((CHUNK, C_WIDTH), row)]
    out_shape = [jax.ShapeDtypeStruct((t, C_WIDTH), BF16)]
    if want_v:
        out_specs.append(pl.BlockSpec((CHUNK, C_WIDTH), row))
        out_shape.append(jax.ShapeDtypeStruct((t, C_WIDTH), F32))
    return pl.pallas_call(
        _gmlp_kernel,
        grid=(t // CHUNK,),
        in_specs=[
            pl.BlockSpec((CHUNK, 3 * C_WIDTH), row),
            pl.BlockSpec((C_GROUPS, CHUNK, CHUNK), lambda i: (0, 0, 0)),
            pl.BlockSpec((CHUNK, C_WIDTH), const),
            pl.BlockSpec((1, C_WIDTH), const),
            pl.BlockSpec((1, C_WIDTH), const),
        ],
        out_specs=out_specs,
        out_shape=out_shape,
        compiler_params=_cparams(("parallel",), 48),
        name="gmlp_mix",
    )(proj, wm, bias, lng.reshape(1, C_WIDTH), lnb.reshape(1, C_WIDTH))


def _pad_cols(w, n):
    return jnp.pad(w, ((0, 0), (0, n - w.shape[1])))


def _rope_tables(pos):
    half = A_ROPE // 2
    inv = ROPE_THETA ** (-jnp.arange(half, dtype=F32) / half)
    ang = pos.astype(F32)[:, None] * inv[None, :]
    cos, sin = jnp.cos(ang), jnp.sin(ang)
    return jnp.tile(cos, (1, LANES // half)), jnp.tile(jnp.concatenate([-sin, sin], axis=1), (1, LANES // A_ROPE))


def _head_expand():
    rows = lax.broadcasted_iota(jnp.int32, (LANES, B_WIDTH), 0)
    cols = lax.broadcasted_iota(jnp.int32, (LANES, B_WIDTH), 1)
    return (cols // B_HEADDIM == rows).astype(BF16)


def _group_reduce():
    rows = lax.broadcasted_iota(jnp.int32, (B_GN, LANES), 0)
    cols = lax.broadcasted_iota(jnp.int32, (B_GN, LANES), 1)
    return ((rows // B_STATE == cols // B_HPG) & (cols < B_HEADS)).astype(BF16)


def kernel(x_prompt, x_sample, c_prompt, c_sample, cache_ckv, cache_kpe, page_table, state_ssm, state_conv, ada_w, ada_b, norm_w, norm_f, a_w_in, a_q_norm, a_w_uq, a_kv_norm, a_w_ukv, a_w_out, b_w_in, b_conv_w, b_conv_b, b_dt_bias, b_a_log, b_d, b_norm, b_w_out, c_w_in, c_ln_g, c_ln_b, c_ws, c_bs, c_w_out):
    bsz, seq, _ = x_prompt.shape
    nb, lq, _ = x_sample.shape
    tp, ts = bsz * seq, nb * lq
    past_len = page_table.shape[1] * PAGE_SIZE

    mods = _ada_call(jnp.concatenate([c_prompt, c_sample], axis=0), ada_w, ada_b)
    e_mat, r_mat = _head_expand(), _group_reduce()
    tril = (lax.broadcasted_iota(jnp.int32, (CHUNK, CHUNK), 0)
            >= lax.broadcasted_iota(jnp.int32, (CHUNK, CHUNK), 1)).astype(BF16)
    cos_p, sin_p = _rope_tables(jnp.tile(jnp.arange(seq, dtype=jnp.int32), bsz))
    cos_s, sin_s = _rope_tables(jnp.tile(past_len + jnp.arange(lq, dtype=jnp.int32), nb))

    cache_kpe_t = cache_kpe.transpose(0, 1, 3, 2)
    xp = x_prompt.reshape(tp, D_MODEL)
    xs = x_sample.reshape(ts, D_MODEL)
    tm_p, tm_s = 1024, ts
    outs = dict(ckv_p=[], kpe_p=[], ckv_s=[], kpe_s=[], ssm_p=[], conv_p=[], ssm_s=[], conv_s=[], v_s=[])

    for l in range(DEPTH):
        kind, j = l % 3, l // 3
        mods_p = mods[l, :bsz].reshape(bsz, 1, 3 * D_MODEL)
        mods_s = jnp.repeat(mods[l, bsz:], lq, axis=0).reshape(1, ts, 3 * D_MODEL)
        if kind == 0:
            o1, o2, o3 = A_Q_LORA, A_Q_LORA + A_KV_LORA, A_Q_LORA + A_KV_LORA + A_ROPE
            w = a_w_in[j]
            w_in = _pad_cols(jnp.concatenate([w[:, o3:], w[:, :o3]], axis=1), A_IN_PAD).astype(BF16)
            wuq = a_w_uq[j].reshape(A_Q_LORA, A_HEADS, A_NOPE + A_ROPE)
            wuq = jnp.concatenate([wuq[:, :, :A_NOPE].reshape(A_Q_LORA, -1),
                                   wuq[:, :, A_NOPE:].reshape(A_Q_LORA, -1)], axis=1).astype(BF16)
            wk = a_w_ukv[j][:, :, :A_NOPE].transpose(1, 2, 0).astype(BF16)
            wv = a_w_ukv[j][:, :, A_NOPE:].transpose(1, 0, 2).astype(BF16)
            w_out = a_w_out[j].astype(BF16)

            proj_p = _in_call(xp, norm_w[l], mods_p, w_in, tm_p, 640, seq)
            qcat, kcat, ckv, kpe = _qkv_call(proj_p, cos_p, sin_p, a_q_norm[j], a_kv_norm[j], wuq, wk, 256)
            a_p = _attn_prompt_call(qcat, kcat, proj_p, wv, bsz, seq, 128, 512, 2)
            xp = _out_call(a_p, w_out, xp, mods_p, tm_p, 512, seq)
            outs["ckv_p"].append(ckv.reshape(bsz, seq, A_KV_LORA))
            outs["kpe_p"].append(kpe[:, :A_ROPE].reshape(bsz, seq, A_ROPE))

            proj_s = _in_call(xs, norm_w[l], mods_s, w_in, tm_s, 640, ts)
            qcat, kcat, ckv, kpe = _qkv_call(proj_s, cos_s, sin_s, a_q_norm[j], a_kv_norm[j], wuq, wk, 256, F32)
            o_lat = _attn_sample_call(page_table, qcat, ckv, kpe, cache_ckv, cache_kpe_t, j, lq, 32, 4)
            a_s = _oproj_call(o_lat, wv, proj_s)
            xs = _out_call(a_s, w_out, xs, mods_s, tm_s, 512, ts)
            outs["ckv_s"].append(ckv.reshape(nb, lq, A_KV_LORA))
            outs["kpe_s"].append(kpe[:, :A_ROPE].reshape(nb, lq, A_ROPE))
        elif kind == 1:
            w_in = _pad_cols(b_w_in[j], B_IN_PAD).astype(BF16)
            w_out = b_w_out[j].astype(BF16)
            cw, cb = b_conv_w[j], b_conv_b[j].reshape(1, B_CONV_DIM)
            dtb = jnp.pad(b_dt_bias[j], (0, LANES - B_HEADS)).reshape(1, LANES)
            alog = jnp.pad(b_a_log[j], (0, LANES - B_HEADS)).reshape(1, LANES)
            dexp = jnp.repeat(b_d[j], B_HEADDIM).reshape(1, B_WIDTH)
            nw = b_norm[j].reshape(1, B_WIDTH)

            proj_p = _in_call(xp, norm_w[l], mods_p, w_in, tm_p, 1152, seq)
            y_p, tail, h_last = _ssd_prompt_call(proj_p, cw, cb, dtb, alog, dexp, nw, e_mat, tril, bsz, seq)
            xp = _out_call(y_p, w_out, xp, mods_p, tm_p, 512, seq)
            outs["ssm_p"].append(h_last)
            outs["conv_p"].append(tail[:, SUBLANES - (B_CONV - 1):])

            proj_s = _in_call(xs, norm_w[l], mods_s, w_in, tm_s, 1152, ts)
            proj_t = proj_s.reshape(nb, lq, B_IN_PAD).transpose(1, 0, 2)
            prev_t = state_conv[j].transpose(1, 0, 2)
            xc = _ssd_sample_conv_call(proj_t, prev_t, cw, cb)
            yin, eax, xw, dec = _ssd_sample_intra_call(xc, proj_t, dtb, alog, dexp, r_mat, e_mat)
            y_s, h_last = _ssd_sample_state_call(dec, state_ssm[j], xc, xw, yin, eax, proj_t, nw)
            xs = _out_call(y_s.reshape(ts, B_WIDTH), w_out, xs, mods_s, tm_s, 512, ts)
            raw = jnp.concatenate([state_conv[j], proj_s[:, B_WIDTH:B_WIDTH + B_CONV_DIM].reshape(nb, lq, B_CONV_DIM)], axis=1)
            outs["ssm_s"].append(h_last)
            outs["conv_s"].append(raw[:, -(B_CONV - 1):])
        else:
            w_in = c_w_in[j].astype(BF16)
            w_out = c_w_out[j].astype(BF16)
            bias_p = jnp.repeat(c_bs[j].T, C_GDIM, axis=1)
            proj_p = _in_call(xp, norm_w[l], mods_p, w_in, tm_p, 1024, seq)
            (m_p,) = _gmlp_call(proj_p, c_ws[j], bias_p, c_ln_g[j], c_ln_b[j], False)
            xp = _out_call(m_p, w_out, xp, mods_p, tm_p, 512, seq)

            reps = CHUNK // lq
            wm_s = jnp.einsum("ab,gts->gatbs", jnp.eye(reps, dtype=F32), c_ws[j][:, :lq, :lq]).reshape(C_GROUPS, CHUNK, CHUNK)
            bias_s = jnp.tile(jnp.repeat(c_bs[j][:, :lq].T, C_GDIM, axis=1), (reps, 1))
            proj_s = _in_call(xs, norm_w[l], mods_s, w_in, tm_s, 1024, ts)
            m_s, v_s = _gmlp_call(proj_s, wm_s, bias_s, c_ln_g[j], c_ln_b[j], True)
            xs = _out_call(m_s, w_out, xs, mods_s, tm_s, 512, ts)
            outs["v_s"].append(v_s.reshape(nb, lq, C_WIDTH))

    y_p = _norm_call(xp, norm_f, 1024).reshape(bsz, seq, D_MODEL)
    y_s = _norm_call(xs, norm_f, ts).reshape(nb, lq, D_MODEL)
    return (y_p, y_s,
            jnp.stack(outs["ckv_p"]), jnp.stack(outs["kpe_p"]), jnp.stack(outs["ckv_s"]), jnp.stack(outs["kpe_s"]),
            jnp.stack(outs["ssm_p"]), jnp.stack(outs["conv_p"]), jnp.stack(outs["ssm_s"]), jnp.stack(outs["conv_s"]),
            jnp.stack(outs["v_s"]))
```

```python
import functools
import math

import jax
import jax.numpy as jnp
from jax import lax
from jax.experimental import pallas as pl
from jax.experimental.pallas import tpu as pltpu

F32 = jnp.float32
BF16 = jnp.bfloat16

D_MODEL = 2048
DEPTH = 4
EPS = 1e-6
A_HEADS = 16
A_NOPE = 128
A_ROPE = 64
A_VDIM = 128
A_Q_LORA = 512
A_KV_LORA = 512
A_WIDTH = A_HEADS * A_VDIM
A_SCALE = (A_NOPE + A_ROPE) ** -0.5
ROPE_THETA = 10000.0
PAGE_SIZE = 128
B_WIDTH = 2 * D_MODEL
B_HEADDIM = 64
B_HEADS = B_WIDTH // B_HEADDIM
B_GROUPS = 8
B_STATE = 128
B_CONV = 4
B_GN = B_GROUPS * B_STATE
B_CONV_DIM = B_WIDTH + 2 * B_GN
B_HPG = B_HEADS // B_GROUPS
B_GW = B_WIDTH // B_GROUPS
CHUNK = 128
C_WIDTH = 2 * D_MODEL
C_GROUPS = 16
C_GDIM = C_WIDTH // C_GROUPS

LANES = 128
SUBLANES = 8
VMEM_BYTES = 64 * 1024 * 1024

A_IN_PAD = 3200
A_QK = A_KV_LORA + LANES
B_IN_PAD = B_WIDTH + B_CONV_DIM + LANES
NEG = -1e30

NT_DIMS = (((1,), (1,)), ((), ()))
TN_DIMS = (((0,), (0,)), ((), ()))


def _cparams(sem, vmem_mb):
    return pltpu.CompilerParams(dimension_semantics=sem, vmem_limit_bytes=vmem_mb * 1024 * 1024)


def _silu(x):
    return x * (1.0 / (1.0 + jnp.exp(-x)))


def _gelu(x):
    return 0.5 * x * (1.0 + jnp.tanh(math.sqrt(2.0 / math.pi) * (x + 0.044715 * (x * x * x))))


def _softplus(x):
    return jnp.maximum(x, 0.0) + jnp.log1p(jnp.exp(-jnp.abs(x)))


def _rms(x, w):
    return x * lax.rsqrt(jnp.mean(x * x, axis=-1, keepdims=True) + EPS) * w


def _dot(a, b):
    return jnp.dot(a, b, preferred_element_type=F32)


def _dot_nt(a, b):
    return lax.dot_general(a, b, NT_DIMS, preferred_element_type=F32)


def _dot_tn(a, b):
    return lax.dot_general(a, b, TN_DIMS, preferred_element_type=F32)


def _split_dot(a, sel, parts, left=False):
    acc = None
    rem = a
    for _ in range(parts):
        piece = rem.astype(BF16)
        rem = rem - piece.astype(F32)
        term = _dot(sel, piece) if left else _dot(piece, sel)
        acc = term if acc is None else acc + term
    return acc


def _ada_kernel(c_ref, w_ref, b_ref, o_ref):
    s = _silu(c_ref[...]).astype(BF16)
    o_ref[0] = _dot(s, w_ref[0].astype(BF16)) + b_ref[0]


def _ada_call(c_all, ada_w, ada_b):
    n_rows = c_all.shape[0]
    tn = 1024
    return pl.pallas_call(
        _ada_kernel,
        grid=(DEPTH, 3 * D_MODEL // tn),
        in_specs=[
            pl.BlockSpec((n_rows, D_MODEL), lambda l, j: (0, 0)),
            pl.BlockSpec((1, D_MODEL, tn), lambda l, j: (l, 0, j)),
            pl.BlockSpec((1, 1, tn), lambda l, j: (l, 0, j)),
        ],
        out_specs=pl.BlockSpec((1, n_rows, tn), lambda l, j: (l, 0, j)),
        out_shape=jax.ShapeDtypeStruct((DEPTH, n_rows, 3 * D_MODEL), F32),
        compiler_params=_cparams(("parallel", "parallel"), 40),
        name="ada_mod",
    )(c_all, ada_w, ada_b.reshape(DEPTH, 1, 3 * D_MODEL))


def _in_kernel(x_ref, nw_ref, sh_ref, sc_ref, w_ref, o_ref, h_ref):
    @pl.when(pl.program_id(1) == 0)
    def _():
        y = _rms(x_ref[...], nw_ref[...])
        h_ref[...] = (y * (1.0 + sc_ref[0]) + sh_ref[0]).astype(BF16)

    o_ref[...] = _dot(h_ref[...], w_ref[...])


def _in_call(x, nw, mods, w, tm, tn, rows_per_mod):
    t, n = x.shape[0], w.shape[1]
    r = mods.shape[1]
    return pl.pallas_call(
        _in_kernel,
        grid=(t // tm, n // tn),
        in_specs=[
            pl.BlockSpec((tm, D_MODEL), lambda i, j: (i, 0)),
            pl.BlockSpec((1, D_MODEL), lambda i, j: (0, 0)),
            pl.BlockSpec((1, r, D_MODEL), lambda i, j: (i * tm // rows_per_mod, 0, 0)),
            pl.BlockSpec((1, r, D_MODEL), lambda i, j: (i * tm // rows_per_mod, 0, 1)),
            pl.BlockSpec((D_MODEL, tn), lambda i, j: (0, j)),
        ],
        out_specs=pl.BlockSpec((tm, tn), lambda i, j: (i, j)),
        out_shape=jax.ShapeDtypeStruct((t, n), F32),
        scratch_shapes=[pltpu.VMEM((tm, D_MODEL), BF16)],
        compiler_params=_cparams(("parallel", "arbitrary"), 56),
        name="in_proj",
    )(x, nw.reshape(1, D_MODEL), mods, mods, w)


def _out_kernel(a_ref, w_ref, x_ref, g_ref, *rest):
    *final_w, o_ref = rest
    y = x_ref[...] + g_ref[0] * _dot(a_ref[...].astype(BF16), w_ref[...])
    o_ref[...] = _rms(y, final_w[0][...]) if final_w else y


def _out_call(a, w, x, mods, tm, tn, rows_per_mod, final_w=None):
    t, k = a.shape
    r = mods.shape[1]
    nj = D_MODEL // tn
    in_specs = [
        pl.BlockSpec((tm, k), lambda i, j: (i, 0)),
        pl.BlockSpec((k, tn), lambda i, j: (0, j)),
        pl.BlockSpec((tm, tn), lambda i, j: (i, j)),
        pl.BlockSpec((1, r, tn), lambda i, j: (i * tm // rows_per_mod, 0, 2 * nj + j)),
    ]
    args = [a, w, x, mods]
    if final_w is not None:
        assert tn == D_MODEL
        in_specs.append(pl.BlockSpec((1, D_MODEL), lambda i, j: (0, 0)))
        args.append(final_w.reshape(1, D_MODEL))
    return pl.pallas_call(
        _out_kernel,
        grid=(t // tm, nj),
        in_specs=in_specs,
        out_specs=pl.BlockSpec((tm, tn), lambda i, j: (i, j)),
        out_shape=jax.ShapeDtypeStruct((t, D_MODEL), F32),
        compiler_params=_cparams(("parallel", "parallel"), 56),
        name="out_proj",
    )(*args)


def _norm_kernel(x_ref, w_ref, o_ref):
    o_ref[...] = _rms(x_ref[...], w_ref[...])


def _norm_call(x, w, tm):
    t = x.shape[0]
    return pl.pallas_call(
        _norm_kernel,
        grid=(t // tm,),
        in_specs=[pl.BlockSpec((tm, D_MODEL), lambda i: (i, 0)), pl.BlockSpec((1, D_MODEL), lambda i: (0, 0))],
        out_specs=pl.BlockSpec((tm, D_MODEL), lambda i: (i, 0)),
        out_shape=jax.ShapeDtypeStruct((t, D_MODEL), F32),
        compiler_params=_cparams(("parallel",), 40),
        name="final_norm",
    )(x, w.reshape(1, D_MODEL))


def _rope(x, cos, ssin):
    lane = lax.broadcasted_iota(jnp.int32, x.shape, 1)
    first = (lane % A_ROPE) < (A_ROPE // 2)
    partner = jnp.where(first, pltpu.roll(x, LANES - A_ROPE // 2, 1), pltpu.roll(x, A_ROPE // 2, 1))
    return x * cos + partner * ssin


def _qkv_kernel(cq_ref, ckv_ref, kpe_ref, cos_ref, sin_ref, qn_ref, kvn_ref, wuq_ref, wk_ref,
                qcat_ref, kcat_ref, ckv_o, kpe_o):
    cos, ssin = cos_ref[...], sin_ref[...]
    ckv = _rms(ckv_ref[...], kvn_ref[...])
    ckv_o[...] = ckv
    kpe = _rope(kpe_ref[...], cos, ssin)
    kpe_o[...] = kpe
    kcat_ref[:, 0:A_KV_LORA] = ckv.astype(BF16)
    kcat_ref[:, A_KV_LORA:A_QK] = kpe.astype(BF16)

    q = _dot(_rms(cq_ref[...], qn_ref[...]).astype(BF16), wuq_ref[...])
    for h in range(A_HEADS):
        qh = q[:, h * A_NOPE:(h + 1) * A_NOPE].astype(BF16)
        qcat_ref[h, :, 0:A_KV_LORA] = (_dot(qh, wk_ref[h]) * A_SCALE).astype(qcat_ref.dtype)
    pe0 = A_HEADS * A_NOPE
    for pr in range(A_HEADS // 2):
        qp = _rope(q[:, pe0 + pr * LANES:pe0 + (pr + 1) * LANES], cos, ssin) * A_SCALE
        qcat_ref[2 * pr, :, A_KV_LORA:A_QK] = qp.astype(qcat_ref.dtype)
        qcat_ref[2 * pr + 1, :, A_KV_LORA:A_QK] = pltpu.roll(qp, A_ROPE, 1).astype(qcat_ref.dtype)


def _qkv_call(proj, cos, ssin, qn, kvn, wuq, wk, tm, q_dtype=BF16):
    t = proj.shape[0]
    cq_blk = D_MODEL // A_Q_LORA
    return pl.pallas_call(
        _qkv_kernel,
        grid=(t // tm,),
        in_specs=[
            pl.BlockSpec((tm, A_Q_LORA), lambda i: (i, cq_blk)),
            pl.BlockSpec((tm, A_KV_LORA), lambda i: (i, cq_blk + 1)),
            pl.BlockSpec((tm, LANES), lambda i: (i, (D_MODEL + A_Q_LORA + A_KV_LORA) // LANES)),
            pl.BlockSpec((tm, LANES), lambda i: (i, 0)),
            pl.BlockSpec((tm, LANES), lambda i: (i, 0)),
            pl.BlockSpec((1, A_Q_LORA), lambda i: (0, 0)),
            pl.BlockSpec((1, A_KV_LORA), lambda i: (0, 0)),
            pl.BlockSpec(wuq.shape, lambda i: (0, 0)),
            pl.BlockSpec(wk.shape, lambda i: (0, 0, 0)),
        ],
        out_specs=[
            pl.BlockSpec((A_HEADS, tm, A_QK), lambda i: (0, i, 0)),
            pl.BlockSpec((tm, A_QK), lambda i: (i, 0)),
            pl.BlockSpec((tm, A_KV_LORA), lambda i: (i, 0)),
            pl.BlockSpec((tm, LANES), lambda i: (i, 0)),
        ],
        out_shape=[
            jax.ShapeDtypeStruct((A_HEADS, t, A_QK), q_dtype),
            jax.ShapeDtypeStruct((t, A_QK), BF16),
            jax.ShapeDtypeStruct((t, A_KV_LORA), F32),
            jax.ShapeDtypeStruct((t, LANES), F32),
        ],
        compiler_params=_cparams(("parallel",), 48),
        name="mla_qkv",
    )(proj, proj, proj, cos, ssin, qn.reshape(1, -1), kvn.reshape(1, -1), wuq, wk)


def _attn_prompt_kernel(q_ref, k_ref, gate_ref, wv_ref, o_ref, *scratch, tq, tk, nk, hg):
    i, j = pl.program_id(1), pl.program_id(2)
    groups = A_HEADS // hg
    m_refs, l_refs, acc_refs = scratch[:groups], scratch[groups:2 * groups], scratch[2 * groups:]

    @pl.when(j == 0)
    def _():
        for g in range(groups):
            m_refs[g][...] = jnp.full(m_refs[g].shape, NEG, F32)
            l_refs[g][...] = jnp.zeros(l_refs[g].shape, F32)
            acc_refs[g][...] = jnp.zeros(acc_refs[g].shape, F32)

    def step(masked, width):
        k = k_ref[0:width, :]
        v = k[:, 0:A_KV_LORA]

        def scores(g):
            return _dot_nt(q_ref[g * hg:(g + 1) * hg].reshape(hg * tq, A_QK), k)

        s_next = scores(0)
        for g in range(groups):
            s = s_next
            if g + 1 < groups:
                s_next = scores(g + 1)
            if masked:
                qpos = i * tq + lax.broadcasted_iota(jnp.int32, (hg, tq, width), 1).reshape(hg * tq, width)
                kpos = j * tk + lax.broadcasted_iota(jnp.int32, (hg * tq, width), 1)
                s = jnp.where(kpos <= qpos, s, NEG)
            m_old = m_refs[g][...]
            m_new = jnp.maximum(m_old, jnp.max(s, axis=-1, keepdims=True))
            p = jnp.exp(s - m_new)
            alpha = jnp.exp(m_old - m_new)
            l_refs[g][...] = alpha * l_refs[g][...] + jnp.sum(p, axis=-1, keepdims=True)
            acc_refs[g][...] = alpha * acc_refs[g][...] + _dot(p.astype(BF16), v)
            m_refs[g][...] = m_new

    first_q, last_q = i * tq, i * tq + tq - 1
    pl.when(j * tk + tk - 1 <= first_q)(lambda: step(False, tk))
    pl.when((j * tk <= last_q) & (j * tk + tk - 1 > first_q))(lambda: step(True, tk))

    @pl.when(j == nk - 1)
    def _():
        for h in range(A_HEADS):
            g, sl = h // hg, slice((h % hg) * tq, (h % hg + 1) * tq)
            o_lat = (acc_refs[g][sl, :] / l_refs[g][sl, :]).astype(BF16)
            cols = slice(h * A_VDIM, (h + 1) * A_VDIM)
            o_ref[:, cols] = (_dot(o_lat, wv_ref[h]) * _silu(gate_ref[:, cols])).astype(BF16)


def _attn_prompt_call(qcat, kcat, proj, wv, bsz, seq, tq, tk, hg=A_HEADS):
    nq, nk = seq // tq, seq // tk
    kern = functools.partial(_attn_prompt_kernel, tq=tq, tk=tk, nk=nk, hg=hg)
    groups, rows = A_HEADS // hg, hg * tq
    scratch = ([pltpu.VMEM((rows, 1), F32)] * (2 * groups)) + ([pltpu.VMEM((rows, A_KV_LORA), F32)] * groups)
    return pl.pallas_call(
        kern,
        grid=(bsz, nq, nk),
        in_specs=[
            pl.BlockSpec((A_HEADS, tq, A_QK), lambda b, i, j: (0, b * nq + i, 0)),
            pl.BlockSpec((tk, A_QK), lambda b, i, j: (b * nk + jnp.minimum(j, (i * tq + tq - 1) // tk), 0)),
            pl.BlockSpec((tq, A_WIDTH), lambda b, i, j: (b * nq + i, 0)),
            pl.BlockSpec(wv.shape, lambda b, i, j: (0, 0, 0)),
        ],
        out_specs=pl.BlockSpec((tq, A_WIDTH), lambda b, i, j: (b * nq + i, 0)),
        out_shape=jax.ShapeDtypeStruct((bsz * seq, A_WIDTH), BF16),
        scratch_shapes=scratch,
        compiler_params=_cparams(("parallel", "parallel", "arbitrary"), 56),
        name="mla_attn_prompt",
    )(qcat, kcat, proj, wv)


def _attn_sample_kernel(pt_ref, q_ref, ckvn_ref, kpen_ref, ckv_hbm, kpe_hbm, o_ref, m_ref, l_ref, acc_ref,
                        kbuf, pbuf, qbuf, knbuf, kraw, praw, sem, *, layer, pages, nsteps, lq, chains):
    b, c = pl.program_id(0), pl.program_id(1)
    rows = qbuf.shape[0]
    toks = [pl.ds((b % SUBLANES) * lq + t, 1) for t in range(lq)]
    step = b * nsteps + c
    slot = step % 2

    def page_copies(seq, chunk, sl):
        cps = []
        for n in range(pages):
            page = 0 if seq is None else pt_ref[seq, chunk * pages + n]
            cps.append(pltpu.make_async_copy(ckv_hbm.at[layer, page], kraw.at[sl, n], sem.at[0, sl]))
            cps.append(pltpu.make_async_copy(kpe_hbm.at[layer, page], praw.at[sl, n], sem.at[1, sl]))
        return cps

    @pl.when(step == 0)
    def _():
        for cp in page_copies(0, 0, 0):
            cp.start()

    @pl.when(step + 1 < pl.num_programs(0) * nsteps)
    def _():
        for cp in page_copies((step + 1) // nsteps, (step + 1) % nsteps, 1 - slot):
            cp.start()

    for cp in page_copies(None, None, slot):
        cp.wait()

    @pl.when(c == 0)
    def _():
        m_ref[...] = jnp.full(m_ref.shape, NEG, F32)
        l_ref[...] = jnp.zeros(l_ref.shape, F32)
        acc_ref[...] = jnp.zeros(acc_ref.shape, F32)
        qbuf[...] = jnp.zeros(qbuf.shape, F32)
        knbuf[...] = jnp.zeros(knbuf.shape, F32)
        for t, tok in enumerate(toks):
            for h in range(A_HEADS):
                qbuf[h * lq + t:h * lq + t + 1, :] = q_ref[h, tok, :]
            knbuf[t:t + 1, 0:A_KV_LORA] = ckvn_ref[tok, :]
            knbuf[t:t + 1, A_KV_LORA:A_QK] = kpen_ref[tok, :]

    q = qbuf[...].astype(BF16)
    ql, qp = q[:, 0:A_KV_LORA], q[:, A_KV_LORA:A_KV_LORA + A_ROPE]

    def as_col(row):
        return jnp.broadcast_to(row, (rows, rows)).T[:, 0:1]

    def local(st, vals):
        m = jnp.max(st, axis=0, keepdims=True)
        pt = jnp.exp(st - m)
        return m, jnp.sum(pt, axis=0, keepdims=True), _dot_tn(pt.astype(BF16), vals)

    def merge(parts):
        m_old = m_ref[...]
        m_new = m_old
        for m, _, _ in parts:
            m_new = jnp.maximum(m_new, m)
        alpha = jnp.exp(m_old - m_new)
        l_new = alpha * l_ref[...]
        acc = as_col(alpha) * acc_ref[...]
        for m, l, o in parts:
            w = jnp.exp(m - m_new)
            l_new = l_new + w * l
            acc = acc + as_col(w) * o
        m_ref[...], l_ref[...], acc_ref[...] = m_new, l_new, acc

    per = pages // chains

    def scores(ch):
        for n in range(ch * per, (ch + 1) * per):
            kbuf[n * PAGE_SIZE:(n + 1) * PAGE_SIZE, :] = kraw[slot, n].astype(BF16)
            pbuf[n * PAGE_SIZE:(n + 1) * PAGE_SIZE, :] = praw[slot, n].T.astype(BF16)
        sl = slice(ch * per * PAGE_SIZE, (ch + 1) * per * PAGE_SIZE)
        keys = kbuf[sl, :]
        return _dot_nt(keys, ql) + _dot_nt(pbuf[sl, :], qp), keys

    parts = []
    nxt = scores(0)
    for ch in range(chains):
        st, keys = nxt
        if ch + 1 < chains:
            nxt = scores(ch + 1)
        parts.append(local(st, keys))
    merge(parts)

    @pl.when(c == nsteps - 1)
    def _():
        kn = knbuf[...].astype(BF16)
        t_k = lax.broadcasted_iota(jnp.int32, (PAGE_SIZE, rows), 0)
        t_q = lax.broadcasted_iota(jnp.int32, (PAGE_SIZE, rows), 1) % lq
        merge([local(jnp.where(t_k <= t_q, _dot_nt(kn, q), NEG), kn[:, 0:A_KV_LORA])])
        o = acc_ref[...] / as_col(l_ref[...])
        for h in range(A_HEADS):
            for t, tok in enumerate(toks):
                o_ref[h, tok, :] = o[h * lq + t:h * lq + t + 1, :]


def _attn_sample_call(page_table, qcat, ckv_new, kpe_new, cache_ckv, cache_kpe_t, layer, lq, pages, chains=2):
    nb, n_pages = page_table.shape
    nsteps = n_pages // pages
    rows = -(-A_HEADS * lq // LANES) * LANES
    blk = SUBLANES * lq
    kern = functools.partial(_attn_sample_kernel, layer=layer, pages=pages, nsteps=nsteps, lq=lq, chains=chains)
    in_specs = [
        pl.BlockSpec((A_HEADS, blk, A_QK), lambda b, c, pt: (0, b // SUBLANES, 0)),
        pl.BlockSpec((blk, A_KV_LORA), lambda b, c, pt: (b // SUBLANES, 0)),
        pl.BlockSpec((blk, LANES), lambda b, c, pt: (b // SUBLANES, 0)),
        pl.BlockSpec(memory_space=pl.ANY),
        pl.BlockSpec(memory_space=pl.ANY),
    ]
    grid_spec = pltpu.PrefetchScalarGridSpec(
        num_scalar_prefetch=1,
        grid=(nb, nsteps),
        in_specs=in_specs,
        out_specs=pl.BlockSpec((A_HEADS, blk, A_KV_LORA), lambda b, c, pt: (0, b // SUBLANES, 0)),
        scratch_shapes=[pltpu.VMEM((1, rows), F32), pltpu.VMEM((1, rows), F32), pltpu.VMEM((rows, A_KV_LORA), F32),
                        pltpu.VMEM((pages * PAGE_SIZE, A_KV_LORA), BF16), pltpu.VMEM((pages * PAGE_SIZE, A_ROPE), BF16),
                        pltpu.VMEM((rows, A_QK), F32), pltpu.VMEM((PAGE_SIZE, A_QK), F32),
                        pltpu.VMEM((2, pages, PAGE_SIZE, A_KV_LORA), F32), pltpu.VMEM((2, pages, A_ROPE, PAGE_SIZE), F32),
                        pltpu.SemaphoreType.DMA((2, 2))],
    )
    return pl.pallas_call(
        kern,
        grid_spec=grid_spec,
        out_shape=jax.ShapeDtypeStruct((A_HEADS, nb * lq, A_KV_LORA), F32),
        compiler_params=_cparams(("arbitrary", "arbitrary"), 48),
        name="mla_attn_sample",
    )(page_table, qcat, ckv_new, kpe_new, cache_ckv, cache_kpe_t)


def _oproj_kernel(o_ref, wv_ref, gate_ref, out_ref):
    out_ref[...] = (_dot(o_ref[0].astype(BF16), wv_ref[0]) * _silu(gate_ref[...])).astype(BF16)


def _oproj_call(o_lat, wv, proj):
    t = o_lat.shape[1]
    return pl.pallas_call(
        _oproj_kernel,
        grid=(A_HEADS,),
        in_specs=[
            pl.BlockSpec((1, t, A_KV_LORA), lambda h: (h, 0, 0)),
            pl.BlockSpec((1, A_KV_LORA, A_VDIM), lambda h: (h, 0, 0)),
            pl.BlockSpec((t, A_VDIM), lambda h: (0, h)),
        ],
        out_specs=pl.BlockSpec((t, A_VDIM), lambda h: (0, h)),
        out_shape=jax.ShapeDtypeStruct((t, A_WIDTH), BF16),
        compiler_params=_cparams(("parallel",), 32),
        name="mla_oproj_sample",
    )(o_lat, wv, proj)


def _group_norm(y, nw):
    outs = []
    for g in range(B_GROUPS):
        sl = slice(g * B_GW, (g + 1) * B_GW)
        outs.append(_rms(y[:, sl], nw[:, sl]))
    return jnp.concatenate(outs, axis=1)


def _ssd_prompt_kernel(proj_ref, cw_ref, cb_ref, dtb_ref, alog_ref, dexp_ref, nw_ref, e_ref, tril_ref,
                       y_ref, tail_ref, hlast_ref, xp_ref, h_ref, yacc_ref, *, nchunks):
    c = pl.program_id(1)
    q = CHUNK

    @pl.when(c == 0)
    def _():
        xp_ref[0:SUBLANES, :] = jnp.zeros((SUBLANES, B_CONV_DIM), F32)
        h_ref[...] = jnp.zeros(h_ref.shape, F32)

    raw = proj_ref[:, B_WIDTH:B_WIDTH + B_CONV_DIM]
    xp_ref[SUBLANES:SUBLANES + q, :] = raw
    conv = cb_ref[...] + cw_ref[3:4, :] * raw
    for back in range(1, B_CONV):
        conv = conv + cw_ref[3 - back:4 - back, :] * xp_ref[SUBLANES - back:SUBLANES - back + q, :]
    xp_ref[0:SUBLANES, :] = raw[q - SUBLANES:q, :]
    tail_ref[0] = raw[q - SUBLANES:q, :]
    xc = _silu(conv)
    x = xc[:, 0:B_WIDTH]
    bm = xc[:, B_WIDTH:B_WIDTH + B_GN].astype(BF16)
    cm = xc[:, B_WIDTH + B_GN:B_CONV_DIM].astype(BF16)

    dt = _softplus(proj_ref[:, B_WIDTH + B_CONV_DIM:B_IN_PAD] + dtb_ref[...])
    dta = dt * (-jnp.exp(alog_ref[...]))
    acum = _split_dot(dta, tril_ref[...], 3, left=True)
    acum_t = acum.T
    alast = acum[q - 1:q, :]
    e = e_ref[...]
    dtx = _split_dot(dt, e, 2)
    eax = _split_dot(jnp.exp(acum), e, 2)
    wx = _split_dot(jnp.exp(alast - acum) * dt, e, 2)
    lane = lax.broadcasted_iota(jnp.int32, (q, B_WIDTH), 1)
    even = (lane % LANES) < B_HEADDIM
    xdt = x * dtx
    xdt_even = jnp.where(even, xdt, 0.0).astype(BF16)
    xdt_odd = jnp.where(even, 0.0, xdt).astype(BF16)
    xw = (x * wx).astype(BF16)
    causal = lax.broadcasted_iota(jnp.int32, (q, q), 0) >= lax.broadcasted_iota(jnp.int32, (q, q), 1)
    dec_col = jnp.exp(acum_t[:, q - 1:q])

    for g in range(B_GROUPS):
        cg = cm[:, g * B_STATE:(g + 1) * B_STATE]
        bg = bm[:, g * B_STATE:(g + 1) * B_STATE]
        cb = _dot_nt(cg, bg)
        hg = h_ref[g * B_HPG:(g + 1) * B_HPG].reshape(B_GW, B_STATE)
        ystate = _dot_nt(cg, hg.astype(BF16))
        for pr in range(B_HPG // 2):
            ls = []
            for hh in (2 * pr, 2 * pr + 1):
                h = g * B_HPG + hh
                diff = acum[:, h:h + 1] - acum_t[h:h + 1, :]
                ls.append((cb * jnp.exp(jnp.where(causal, diff, NEG))).astype(BF16))
            col = g * B_GW + pr * LANES
            lhs = jnp.concatenate(ls, axis=1)
            rhs = jnp.concatenate([xdt_even[:, col:col + LANES], xdt_odd[:, col:col + LANES]], axis=0)
            yacc_ref[:, col:col + LANES] = _dot(lhs, rhs) + ystate[:, pr * LANES:(pr + 1) * LANES] * eax[:, col:col + LANES]
        snew = _dot_tn(xw[:, g * B_GW:(g + 1) * B_GW], bg)
        for hh in range(B_HPG):
            h = g * B_HPG + hh
            dec = jnp.broadcast_to(dec_col[h:h + 1, :], (B_HEADDIM, B_STATE))
            h_ref[h] = h_ref[h] * dec + snew[hh * B_HEADDIM:(hh + 1) * B_HEADDIM, :]

    y = (yacc_ref[...] + dexp_ref[...] * x) * _silu(proj_ref[:, 0:B_WIDTH])
    y_ref[...] = _group_norm(y, nw_ref[...]).astype(BF16)

    @pl.when(c == nchunks - 1)
    def _():
        hlast_ref[0] = h_ref[...]


def _ssd_prompt_call(proj, cw, cb, dtb, alog, dexp, nw, e, tril, bsz, seq):
    nchunks = seq // CHUNK
    kern = functools.partial(_ssd_prompt_kernel, nchunks=nchunks)
    const = lambda b, c: (0, 0)
    return pl.pallas_call(
        kern,
        grid=(bsz, nchunks),
        in_specs=[
            pl.BlockSpec((CHUNK, B_IN_PAD), lambda b, c: (b * nchunks + c, 0)),
            pl.BlockSpec((B_CONV, B_CONV_DIM), const),
            pl.BlockSpec((1, B_CONV_DIM), const),
            pl.BlockSpec((1, LANES), const),
            pl.BlockSpec((1, LANES), const),
            pl.BlockSpec((1, B_WIDTH), const),
            pl.BlockSpec((1, B_WIDTH), const),
            pl.BlockSpec((LANES, B_WIDTH), const),
            pl.BlockSpec((CHUNK, CHUNK), const),
        ],
        out_specs=[
            pl.BlockSpec((CHUNK, B_WIDTH), lambda b, c: (b * nchunks + c, 0)),
            pl.BlockSpec((1, SUBLANES, B_CONV_DIM), lambda b, c: (b, 0, 0)),
            pl.BlockSpec((1, B_HEADS, B_HEADDIM, B_STATE), lambda b, c: (b, 0, 0, 0)),
        ],
        out_shape=[
            jax.ShapeDtypeStruct((bsz * seq, B_WIDTH), BF16),
            jax.ShapeDtypeStruct((bsz, SUBLANES, B_CONV_DIM), F32),
            jax.ShapeDtypeStruct((bsz, B_HEADS, B_HEADDIM, B_STATE), F32),
        ],
        scratch_shapes=[
            pltpu.VMEM((SUBLANES + CHUNK, B_CONV_DIM), F32),
            pltpu.VMEM((B_HEADS, B_HEADDIM, B_STATE), F32),
            pltpu.VMEM((CHUNK, B_WIDTH), F32),
        ],
        compiler_params=_cparams(("parallel", "arbitrary"), 56),
        name="ssd_prompt",
    )(proj, cw, cb, dtb, alog, dexp, nw, e, tril)


def _ssd_sample_conv_kernel(x_ref, prev_ref, cw_ref, cb_ref, o_ref, *, lq):
    rows = [prev_ref[n] for n in range(B_CONV - 1)] + [x_ref[n] for n in range(lq)]
    for t in range(lq):
        conv = cb_ref[...]
        for j in range(B_CONV):
            conv = conv + cw_ref[j:j + 1, :] * rows[t + j]
        o_ref[t] = _silu(conv)


def _ssd_sample_conv_call(proj_t, prev_t, cw, cb):
    lq, nb = proj_t.shape[:2]
    tc = 2048
    off = B_WIDTH // tc
    kern = functools.partial(_ssd_sample_conv_kernel, lq=lq)
    return pl.pallas_call(
        kern,
        grid=(B_CONV_DIM // tc,),
        in_specs=[
            pl.BlockSpec((lq, nb, tc), lambda j: (0, 0, off + j)),
            pl.BlockSpec((B_CONV - 1, nb, tc), lambda j: (0, 0, j)),
            pl.BlockSpec((B_CONV, tc), lambda j: (0, j)),
            pl.BlockSpec((1, tc), lambda j: (0, j)),
        ],
        out_specs=pl.BlockSpec((lq, nb, tc), lambda j: (0, 0, j)),
        out_shape=jax.ShapeDtypeStruct((lq, nb, B_CONV_DIM), F32),
        compiler_params=_cparams(("parallel",), 40),
        name="ssd_sample_conv",
    )(proj_t, prev_t, cw, cb)


def _ssd_sample_intra_kernel(xc_ref, dt_ref, dtb_ref, alog_ref, dexp_ref, r_ref, e_ref,
                             yin_ref, eax_ref, xw_ref, dec_ref, *, lq):
    qi = pl.program_id(0)
    a = -jnp.exp(alog_ref[...])
    dts = [_softplus(dt_ref[k] + dtb_ref[...]) for k in range(lq)]
    acums = []
    run = None
    for k in range(lq):
        run = dts[k] * a if run is None else run + dts[k] * a
        acums.append(run)
    acum_q = acums[0]
    for k in range(1, lq):
        acum_q = jnp.where(qi >= k, acums[k], acum_q)
    e, r = e_ref[...], r_ref[...]
    x_q = xc_ref[qi, :, 0:B_WIDTH]
    c_q = xc_ref[qi, :, B_WIDTH + B_GN:B_CONV_DIM]
    y = dexp_ref[...] * x_q
    for k in range(lq):
        b_k = xc_ref[k, :, B_WIDTH:B_WIDTH + B_GN]
        cb = _split_dot(c_q * b_k, r, 2)
        m = cb * jnp.exp(jnp.where(qi >= k, acum_q - acums[k], NEG)) * dts[k]
        y = y + _split_dot(m, e, 2) * xc_ref[k, :, 0:B_WIDTH]
    yin_ref[0] = y
    eax_ref[0] = _split_dot(jnp.exp(acum_q), e, 2)
    dt_q = dts[0]
    for k in range(1, lq):
        dt_q = jnp.where(qi == k, dts[k], dt_q)
    xw_ref[0] = x_q * _split_dot(jnp.exp(acums[lq - 1] - acum_q) * dt_q, e, 2)
    dec_ref[...] = jnp.exp(acums[lq - 1])


def _ssd_sample_intra_call(xc, proj_t, dtb, alog, dexp, r, e):
    lq, nb = xc.shape[:2]
    kern = functools.partial(_ssd_sample_intra_kernel, lq=lq)
    tok = lambda q: (q, 0, 0)
    const = lambda q: (0, 0)
    big = jax.ShapeDtypeStruct((lq, nb, B_WIDTH), F32)
    return pl.pallas_call(
        kern,
        grid=(lq,),
        in_specs=[
            pl.BlockSpec((lq, nb, B_CONV_DIM), lambda q: (0, 0, 0)),
            pl.BlockSpec((lq, nb, LANES), lambda q: (0, 0, (B_WIDTH + B_CONV_DIM) // LANES)),
            pl.BlockSpec((1, LANES), const),
            pl.BlockSpec((1, LANES), const),
            pl.BlockSpec((1, B_WIDTH), const),
            pl.BlockSpec((B_GN, LANES), const),
            pl.BlockSpec((LANES, B_WIDTH), const),
        ],
        out_specs=[
            pl.BlockSpec((1, nb, B_WIDTH), tok),
            pl.BlockSpec((1, nb, B_WIDTH), tok),
            pl.BlockSpec((1, nb, B_WIDTH), tok),
            pl.BlockSpec((nb, LANES), const),
        ],
        out_shape=[big, big, big, jax.ShapeDtypeStruct((nb, LANES), F32)],
        compiler_params=_cparams(("arbitrary",), 56),
        name="ssd_sample_intra",
    )(xc, proj_t, dtb, alog, dexp, r, e)


def _ssd_sample_state_kernel(dec_ref, h0_ref, c_ref, b_ref, xw_ref, yin_ref, eax_ref, z_ref, nw_ref,
                             y_ref, hl_ref, c8, b8, xw8, *, lq):
    s = pl.program_id(0)
    r = pl.ds(s % SUBLANES, 1)
    for buf in (c8, b8, xw8):
        buf[lq:SUBLANES, :] = jnp.zeros((SUBLANES - lq, buf.shape[1]), F32)
    for t in range(lq):
        c8[t:t + 1, :] = c_ref[t, r, :]
        b8[t:t + 1, :] = b_ref[t, r, :]
        xw8[t:t + 1, :] = xw_ref[t, r, :]
    ys = []
    for g in range(B_GROUPS):
        hg = h0_ref[0, g * B_HPG:(g + 1) * B_HPG].reshape(B_GW, B_STATE).astype(BF16)
        cg = c8[:, g * B_STATE:(g + 1) * B_STATE].astype(BF16)
        bg = b8[:, g * B_STATE:(g + 1) * B_STATE].astype(BF16)
        ys.append(_dot_nt(cg, hg))
        snew = _dot_tn(xw8[:, g * B_GW:(g + 1) * B_GW].astype(BF16), bg)
        for hh in range(B_HPG):
            h = g * B_HPG + hh
            hl_ref[0, h] = h0_ref[0, h] * dec_ref[s, h] + snew[hh * B_HEADDIM:(hh + 1) * B_HEADDIM, :]
    ystate = jnp.concatenate(ys, axis=1)
    for t in range(lq):
        y = (yin_ref[t, r, :] + ystate[t:t + 1, :] * eax_ref[t, r, :]) * _silu(z_ref[t, r, :])
        y_ref[0, t:t + 1, :] = _group_norm(y, nw_ref[...])


def _ssd_sample_state_call(dec, h0, xc, xw, yin, eax, proj_t, nw):
    lq, nb = xc.shape[:2]
    kern = functools.partial(_ssd_sample_state_kernel, lq=lq)
    row = lambda blk: (lambda s, d: (0, s // SUBLANES, blk))
    grid_spec = pltpu.PrefetchScalarGridSpec(
        num_scalar_prefetch=1,
        grid=(nb,),
        in_specs=[
            pl.BlockSpec((1, B_HEADS, B_HEADDIM, B_STATE), lambda s, d: (s, 0, 0, 0)),
            pl.BlockSpec((lq, SUBLANES, B_GN), row((B_WIDTH + B_GN) // B_GN)),
            pl.BlockSpec((lq, SUBLANES, B_GN), row(B_WIDTH // B_GN)),
            pl.BlockSpec((lq, SUBLANES, B_WIDTH), row(0)),
            pl.BlockSpec((lq, SUBLANES, B_WIDTH), row(0)),
            pl.BlockSpec((lq, SUBLANES, B_WIDTH), row(0)),
            pl.BlockSpec((lq, SUBLANES, B_WIDTH), row(0)),
            pl.BlockSpec((1, B_WIDTH), lambda s, d: (0, 0)),
        ],
        out_specs=[
            pl.BlockSpec((1, lq, B_WIDTH), lambda s, d: (s, 0, 0)),
            pl.BlockSpec((1, B_HEADS, B_HEADDIM, B_STATE), lambda s, d: (s, 0, 0, 0)),
        ],
        scratch_shapes=[pltpu.VMEM((SUBLANES, B_GN), F32), pltpu.VMEM((SUBLANES, B_GN), F32),
                        pltpu.VMEM((SUBLANES, B_WIDTH), F32)],
    )
    return pl.pallas_call(
        kern,
        grid_spec=grid_spec,
        out_shape=[jax.ShapeDtypeStruct((nb, lq, B_WIDTH), F32),
                   jax.ShapeDtypeStruct((nb, B_HEADS, B_HEADDIM, B_STATE), F32)],
        compiler_params=_cparams(("arbitrary",), 40),
        name="ssd_sample_state",
    )(dec, h0, xc, xc, xw, yin, eax, proj_t, nw)


def _gmlp_kernel(p_ref, wm_ref, bias_ref, lng_ref, lnb_ref, o_ref, *v_refs):
    q = CHUNK
    u = _gelu(p_ref[:, 0:C_WIDTH])
    vp = _gelu(p_ref[:, C_WIDTH:2 * C_WIDTH])
    mu = jnp.mean(vp, axis=-1, keepdims=True)
    var = jnp.mean(jnp.square(vp - mu), axis=-1, keepdims=True)
    v = (vp - mu) * lax.rsqrt(var + EPS) * lng_ref[...] + lnb_ref[...]
    if v_refs:
        v_refs[0][...] = v
    vb = v.astype(BF16)
    causal = lax.broadcasted_iota(jnp.int32, (q, q), 0) >= lax.broadcasted_iota(jnp.int32, (q, q), 1)
    for g in range(C_GROUPS):
        sl = slice(g * C_GDIM, (g + 1) * C_GDIM)
        w = jnp.where(causal, wm_ref[g], 0.0).astype(BF16)
        mix = _dot(w, vb[:, sl]) + bias_ref[:, sl]
        gate = p_ref[:, 2 * C_WIDTH + g * C_GDIM:2 * C_WIDTH + (g + 1) * C_GDIM]
        o_ref[:, sl] = (u[:, sl] * mix * _silu(gate)).astype(BF16)


def _gmlp_call(proj, wm, bias, lng, lnb, want_v):
    t = proj.shape[0]
    row = lambda i: (i, 0)
    const = lambda i: (0, 0)
    out_specs = [pl.BlockSpec((CHUNK, C_WIDTH), row)]
    out_shape = [jax.ShapeDtypeStruct((t, C_WIDTH), BF16)]
    if want_v:
        out_specs.append(pl.BlockSpec((CHUNK, C_WIDTH), row))
        out_shape.append(jax.ShapeDtypeStruct((t, C_WIDTH), F32))
    return pl.pallas_call(
        _gmlp_kernel,
        grid=(t // CHUNK,),
        in_specs=[
            pl.BlockSpec((CHUNK, 3 * C_WIDTH), row),
            pl.BlockSpec((C_GROUPS, CHUNK, CHUNK), lambda i: (0, 0, 0)),
            pl.BlockSpec((CHUNK, C_WIDTH), const),
            pl.BlockSpec((1, C_WIDTH), const),
            pl.BlockSpec((1, C_WIDTH), const),
        ],
        out_specs=out_specs,
        out_shape=out_shape,
        compiler_params=_cparams(("parallel",), 48),
        name="gmlp_mix",
    )(proj, wm, bias, lng.reshape(1, C_WIDTH), lnb.reshape(1, C_WIDTH))


def _pad_cols(w, n):
    return jnp.pad(w, ((0, 0), (0, n - w.shape[1])))


def _rope_tables(pos):
    half = A_ROPE // 2
    inv = ROPE_THETA ** (-jnp.arange(half, dtype=F32) / half)
    ang = pos.astype(F32)[:, None] * inv[None, :]
    cos, sin = jnp.cos(ang), jnp.sin(ang)
    return jnp.tile(cos, (1, LANES // half)), jnp.tile(jnp.concatenate([-sin, sin], axis=1), (1, LANES // A_ROPE))


def _head_expand():
    rows = lax.broadcasted_iota(jnp.int32, (LANES, B_WIDTH), 0)
    cols = lax.broadcasted_iota(jnp.int32, (LANES, B_WIDTH), 1)
    return (cols // B_HEADDIM == rows).astype(BF16)


def _group_reduce():
    rows = lax.broadcasted_iota(jnp.int32, (B_GN, LANES), 0)
    cols = lax.broadcasted_iota(jnp.int32, (B_GN, LANES), 1)
    return ((rows // B_STATE == cols // B_HPG) & (cols < B_HEADS)).astype(BF16)


def kernel(x_prompt, x_sample, c_prompt, c_sample, cache_ckv, cache_kpe, page_table, state_ssm, state_conv, ada_w, ada_b, norm_w, norm_f, a_w_in, a_q_norm, a_w_uq, a_kv_norm, a_w_ukv, a_w_out, b_w_in, b_conv_w, b_conv_b, b_dt_bias, b_a_log, b_d, b_norm, b_w_out, c_w_in, c_ln_g, c_ln_b, c_ws, c_bs, c_w_out):
    bsz, seq, _ = x_prompt.shape
    nb, lq, _ = x_sample.shape
    tp, ts = bsz * seq, nb * lq
    past_len = page_table.shape[1] * PAGE_SIZE

    mods = _ada_call(jnp.concatenate([c_prompt, c_sample], axis=0), ada_w, ada_b)
    e_mat, r_mat = _head_expand(), _group_reduce()
    tril = (lax.broadcasted_iota(jnp.int32, (CHUNK, CHUNK), 0)
            >= lax.broadcasted_iota(jnp.int32, (CHUNK, CHUNK), 1)).astype(BF16)
    cos_p, sin_p = _rope_tables(jnp.tile(jnp.arange(seq, dtype=jnp.int32), bsz))
    cos_s, sin_s = _rope_tables(jnp.tile(past_len + jnp.arange(lq, dtype=jnp.int32), nb))

    cache_kpe_t = cache_kpe.transpose(0, 1, 3, 2)
    xp = x_prompt.reshape(tp, D_MODEL)
    xs = x_sample.reshape(ts, D_MODEL)
    tm_p, tm_s = 1024, ts
    outs = dict(ckv_p=[], kpe_p=[], ckv_s=[], kpe_s=[], ssm_p=[], conv_p=[], ssm_s=[], conv_s=[], v_s=[])

    def residual(a, w, x, mods, tm, rows_per_mod, last):
        if not last:
            return _out_call(a, w, x, mods, tm, 512, rows_per_mod)
        if a.shape[1] > D_MODEL:
            return _norm_call(_out_call(a, w, x, mods, tm, 512, rows_per_mod), norm_f, tm)
        return _out_call(a, w, x, mods, min(tm, 512), D_MODEL, rows_per_mod, norm_f)

    for l in range(DEPTH):
        kind, j = l % 3, l // 3
        last = l == DEPTH - 1
        mods_p = mods[l, :bsz].reshape(bsz, 1, 3 * D_MODEL)
        mods_s = jnp.repeat(mods[l, bsz:], lq, axis=0).reshape(1, ts, 3 * D_MODEL)
        if kind == 0:
            o1, o2, o3 = A_Q_LORA, A_Q_LORA + A_KV_LORA, A_Q_LORA + A_KV_LORA + A_ROPE
            w = a_w_in[j]
            w_in = _pad_cols(jnp.concatenate([w[:, o3:], w[:, :o3]], axis=1), A_IN_PAD).astype(BF16)
            wuq = a_w_uq[j].reshape(A_Q_LORA, A_HEADS, A_NOPE + A_ROPE)
            wuq = jnp.concatenate([wuq[:, :, :A_NOPE].reshape(A_Q_LORA, -1),
                                   wuq[:, :, A_NOPE:].reshape(A_Q_LORA, -1)], axis=1).astype(BF16)
            wk = a_w_ukv[j][:, :, :A_NOPE].transpose(1, 2, 0).astype(BF16)
            wv = a_w_ukv[j][:, :, A_NOPE:].transpose(1, 0, 2).astype(BF16)
            w_out = a_w_out[j].astype(BF16)

            proj_p = _in_call(xp, norm_w[l], mods_p, w_in, 512, A_IN_PAD, seq)
            qcat, kcat, ckv, kpe = _qkv_call(proj_p, cos_p, sin_p, a_q_norm[j], a_kv_norm[j], wuq, wk, 256)
            a_p = _attn_prompt_call(qcat, kcat, proj_p, wv, bsz, seq, 128, 512, 2)
            xp = residual(a_p, w_out, xp, mods_p, tm_p, seq, last)
            outs["ckv_p"].append(ckv.reshape(bsz, seq, A_KV_LORA))
            outs["kpe_p"].append(kpe[:, :A_ROPE].reshape(bsz, seq, A_ROPE))

            proj_s = _in_call(xs, norm_w[l], mods_s, w_in, tm_s, A_IN_PAD, ts)
            qcat, kcat, ckv, kpe = _qkv_call(proj_s, cos_s, sin_s, a_q_norm[j], a_kv_norm[j], wuq, wk, 256, F32)
            o_lat = _attn_sample_call(page_table, qcat, ckv, kpe, cache_ckv, cache_kpe_t, j, lq, 32, 4)
            a_s = _oproj_call(o_lat, wv, proj_s)
            xs = residual(a_s, w_out, xs, mods_s, tm_s, ts, last)
            outs["ckv_s"].append(ckv.reshape(nb, lq, A_KV_LORA))
            outs["kpe_s"].append(kpe[:, :A_ROPE].reshape(nb, lq, A_ROPE))
        elif kind == 1:
            w_in = _pad_cols(b_w_in[j], B_IN_PAD).astype(BF16)
            w_out = b_w_out[j].astype(BF16)
            cw, cb = b_conv_w[j], b_conv_b[j].reshape(1, B_CONV_DIM)
            dtb = jnp.pad(b_dt_bias[j], (0, LANES - B_HEADS)).reshape(1, LANES)
            alog = jnp.pad(b_a_log[j], (0, LANES - B_HEADS)).reshape(1, LANES)
            dexp = jnp.repeat(b_d[j], B_HEADDIM).reshape(1, B_WIDTH)
            nw = b_norm[j].reshape(1, B_WIDTH)

            proj_p = _in_call(xp, norm_w[l], mods_p, w_in, tm_p, 1152, seq)
            y_p, tail, h_last = _ssd_prompt_call(proj_p, cw, cb, dtb, alog, dexp, nw, e_mat, tril, bsz, seq)
            xp = residual(y_p, w_out, xp, mods_p, tm_p, seq, last)
            outs["ssm_p"].append(h_last)
            outs["conv_p"].append(tail[:, SUBLANES - (B_CONV - 1):])

            proj_s = _in_call(xs, norm_w[l], mods_s, w_in, tm_s, 1152, ts)
            proj_t = proj_s.reshape(nb, lq, B_IN_PAD).transpose(1, 0, 2)
            prev_t = state_conv[j].transpose(1, 0, 2)
            xc = _ssd_sample_conv_call(proj_t, prev_t, cw, cb)
            yin, eax, xw, dec = _ssd_sample_intra_call(xc, proj_t, dtb, alog, dexp, r_mat, e_mat)
            y_s, h_last = _ssd_sample_state_call(dec, state_ssm[j], xc, xw, yin, eax, proj_t, nw)
            xs = residual(y_s.reshape(ts, B_WIDTH), w_out, xs, mods_s, tm_s, ts, last)
            raw = jnp.concatenate([state_conv[j], proj_s[:, B_WIDTH:B_WIDTH + B_CONV_DIM].reshape(nb, lq, B_CONV_DIM)], axis=1)
            outs["ssm_s"].append(h_last)
            outs["conv_s"].append(raw[:, -(B_CONV - 1):])
        else:
            w_in = c_w_in[j].astype(BF16)
            w_out = c_w_out[j].astype(BF16)
            bias_p = jnp.repeat(c_bs[j].T, C_GDIM, axis=1)
            proj_p = _in_call(xp, norm_w[l], mods_p, w_in, tm_p, 1024, seq)
            (m_p,) = _gmlp_call(proj_p, c_ws[j], bias_p, c_ln_g[j], c_ln_b[j], False)
            xp = residual(m_p, w_out, xp, mods_p, tm_p, seq, last)

            reps = CHUNK // lq
            wm_s = jnp.einsum("ab,gts->gatbs", jnp.eye(reps, dtype=F32), c_ws[j][:, :lq, :lq]).reshape(C_GROUPS, CHUNK, CHUNK)
            bias_s = jnp.tile(jnp.repeat(c_bs[j][:, :lq].T, C_GDIM, axis=1), (reps, 1))
            proj_s = _in_call(xs, norm_w[l], mods_s, w_in, tm_s, 1024, ts)
            m_s, v_s = _gmlp_call(proj_s, wm_s, bias_s, c_ln_g[j], c_ln_b[j], True)
            xs = residual(m_s, w_out, xs, mods_s, tm_s, ts, last)
            outs["v_s"].append(v_s.reshape(nb, lq, C_WIDTH))

    y_p = xp.reshape(bsz, seq, D_MODEL)
    y_s = xs.reshape(nb, lq, D_MODEL)
    return (y_p, y_s,
            jnp.stack(outs["ckv_p"]), jnp.stack(outs["kpe_p"]), jnp.stack(outs["ckv_s"]), jnp.stack(outs["kpe_s"]),
            jnp.stack(outs["ssm_p"]), jnp.stack(outs["conv_p"]), jnp.stack(outs["ssm_s"]), jnp.stack(outs["conv_s"]),
            jnp.stack(outs["v_s"]))
```
